```python
import jax, jax.numpy as jnp
from jax import lax
import numpy as np

D_MODEL = 1024
BATCH = 4
SEQ = 8192
DEPTH = 4

GRID_W = 64
CTX_LEN = 256
HG_HEADS = 4
HG_DK = 128
HG_DV = 128
GLA_HEADS = 4
GLA_DK = 64
GLA_DV = 128
GLA_GATE_RANK = 16
GLA_GATE_NORM = 16.0
HG_KW = HG_HEADS * HG_DK
HG_VW = HG_HEADS * HG_DV
GLA_KW = GLA_HEADS * GLA_DK
GLA_VW = GLA_HEADS * GLA_DV
D_MIX = HG_VW + GLA_VW
CHUNK = 64
PEER_HEADS = 8
PEER_NKEYS = 128
PEER_EXPERTS = PEER_NKEYS * PEER_NKEYS
PEER_DQ = 256
PEER_TOPK = 16
PEER_BLOCK = 128
ALPHA = (2.0 * DEPTH) ** 0.25
BETA = (8.0 * DEPTH) ** -0.25
EPS = 1e-6
LB_FLOOR = 1e-30
IN_SPLITS = (HG_KW, HG_KW, HG_KW, HG_VW, HG_VW, GLA_KW, GLA_KW, GLA_VW, GLA_VW, GLA_GATE_RANK, GLA_GATE_RANK)
D_IN = sum(IN_SPLITS)
IN_SPLIT_POINTS = [int(p) for p in np.cumsum(IN_SPLITS)[:-1]]

kernel_name = 'hybrid_hgrn2_gla_peer_dit'


def layer_norm(x, gamma=None, beta=None):
    xf = x.astype(jnp.float32)
    mu = jnp.mean(xf, axis=-1, keepdims=True)
    var = jnp.mean(jnp.square(xf - mu), axis=-1, keepdims=True)
    y = (xf - mu) * lax.rsqrt(var + EPS)
    if gamma is not None:
        y = y * gamma.astype(jnp.float32) + beta.astype(jnp.float32)
    return y.astype(x.dtype)


def modulate(x, shift, scale):
    return layer_norm(x) * (1.0 + scale) + shift


def heads(t, n):
    B, L, _ = t.shape
    return t.reshape(B, L, n, -1).transpose(0, 2, 1, 3).astype(jnp.float32)


def head_rms_merge(o, gain):
    o = o * lax.rsqrt(jnp.mean(o * o, axis=-1, keepdims=True) + EPS) * gain.astype(jnp.float32)
    B, n, L, d = o.shape
    return o.transpose(0, 2, 1, 3).reshape(B, L, n * d)


def to_col_major(t, rows):
    B, L, C = t.shape
    return t.reshape(B, rows, GRID_W, C).transpose(0, 2, 1, 3).reshape(B, L, C)


def from_col_major(t, rows):
    B, L, C = t.shape
    return t.reshape(B, GRID_W, rows, C).transpose(0, 2, 1, 3).reshape(B, L, C)


def chunk_scan(q, k, v, log_a, s0):
    B, H, L, dk = q.shape
    dv = v.shape[-1]
    n = L // CHUNK

    def split(t):
        return jnp.moveaxis(t.reshape(B, H, n, CHUNK, t.shape[-1]), 2, 0)

    incl = jnp.tril(jnp.ones((CHUNK, CHUNK), dtype=bool))[:, :, None]

    def step(s, inp):
        qc, kc, vc, ac = inp
        b = jnp.cumsum(ac, axis=-2)
        rel = b[..., :, None, :] - b[..., None, :, :]
        decay = jnp.where(incl, jnp.exp(jnp.minimum(rel, 0.0)), 0.0)
        scores = jnp.einsum('bhtk,bhsk,bhtsk->bhts', qc, kc, decay)
        o = jnp.einsum('bhts,bhsv->bhtv', scores, vc) + jnp.einsum('bhtk,bhkv->bhtv', qc * jnp.exp(b), s)
        b_end = b[..., -1:, :]
        s_new = jnp.exp(b_end[..., 0, :])[..., None] * s + jnp.einsum('bhsk,bhsv->bhkv', kc * jnp.exp(b_end - b), vc)
        return s_new, o

    s_end, o = lax.scan(step, s0, (split(q), split(k), split(v), split(log_a)))
    return jnp.moveaxis(o, 0, 2).reshape(B, H, L, dv), s_end


def scan_two_way(ctx_parts, lat_parts):
    qc, vc, kcf, acf, kcb, acb = ctx_parts
    ql, vl, klf, alf, klb, alb = lat_parts
    B, H, _, dk = qc.shape
    s0 = jnp.zeros((B, H, dk, vc.shape[-1]), jnp.float32)
    flip = lambda t: jnp.flip(t, axis=2)
    oc_f, sc_f = chunk_scan(qc, kcf, vc, acf, s0)
    ol_f, _ = chunk_scan(ql, klf, vl, alf, sc_f)
    oc_b, sc_b = chunk_scan(flip(qc), flip(kcb), flip(vc), flip(acb), s0)
    ol_b, _ = chunk_scan(flip(ql), flip(klb), flip(vl), flip(alb), sc_b)
    return oc_f + flip(oc_b), ol_f + flip(ol_b)


def hgrn2_inputs(hq, hff, hfb, hi, lb_f, lb_b):
    def gate(z, lb):
        z = z.astype(jnp.float32)
        log_lb = jnp.log(jnp.maximum(lb, LB_FLOOR))
        log_f = jnp.logaddexp(log_lb, jnp.log1p(-lb) + jax.nn.log_sigmoid(z))
        k = (1.0 - lb) * jax.nn.sigmoid(-z)
        return heads(k, HG_HEADS), heads(log_f, HG_HEADS)
    kf, af = gate(hff, lb_f)
    kb, ab = gate(hfb, lb_b)
    return (heads(hq, HG_HEADS), heads(hi, HG_HEADS), kf, af, kb, ab)


def gla_inputs(gq, gk, gv, glf, glb, w_gk2, b_gk):
    def gate(r, w, b):
        la = jax.nn.log_sigmoid((r @ w + b).astype(jnp.float32)) / GLA_GATE_NORM
        return heads(la, GLA_HEADS)
    k = heads(gk, GLA_HEADS)
    q = heads(gq, GLA_HEADS) * (GLA_DK ** -0.5)
    return (q, heads(gv, GLA_HEADS), k, gate(glf, w_gk2[0], b_gk[0]), k, gate(glb, w_gk2[1], b_gk[1]))


def token_mixer(u_c, u_l, rows, lb_f, lb_b, w_gk2, b_gk, hg_gain, gla_gain):
    pc = jnp.split(u_c, IN_SPLIT_POINTS, axis=-1)
    pl = jnp.split(u_l, IN_SPLIT_POINTS, axis=-1)
    a_c, a_l = scan_two_way(hgrn2_inputs(pc[0], pc[1], pc[2], pc[3], lb_f, lb_b),
                            hgrn2_inputs(pl[0], pl[1], pl[2], pl[3], lb_f, lb_b))
    out_a_c = head_rms_merge(a_c, hg_gain).astype(u_c.dtype) * jax.nn.silu(pc[4])
    out_a_l = head_rms_merge(a_l, hg_gain).astype(u_l.dtype) * jax.nn.silu(pl[4])
    gl = [to_col_major(pl[j], rows) for j in (5, 6, 7, 9, 10)]
    b_c, b_l = scan_two_way(gla_inputs(pc[5], pc[6], pc[7], pc[9], pc[10], w_gk2, b_gk),
                            gla_inputs(gl[0], gl[1], gl[2], gl[3], gl[4], w_gk2, b_gk))
    out_b_c = head_rms_merge(b_c, gla_gain).astype(u_c.dtype) * jax.nn.silu(pc[8])
    out_b_l = from_col_major(head_rms_merge(b_l, gla_gain), rows).astype(u_l.dtype) * jax.nn.silu(pl[8])
    return (jnp.concatenate([out_a_c, out_b_c], axis=-1), jnp.concatenate([out_a_l, out_b_l], axis=-1))


def peer(h, w_query, sub_keys, expert_u, expert_v):
    B, L, D = h.shape
    tok = h.reshape(-1, PEER_BLOCK, D)
    k1 = sub_keys[0].astype(jnp.float32)
    k2 = sub_keys[1].astype(jnp.float32)

    def block(t):
        q = (t @ w_query).reshape(PEER_BLOCK, PEER_HEADS, 2, PEER_DQ // 2).astype(jnp.float32)
        s1 = jnp.einsum('thd,nd->thn', q[:, :, 0], k1)
        s2 = jnp.einsum('thd,nd->thn', q[:, :, 1], k2)
        v1, i1 = lax.top_k(s1, PEER_TOPK)
        v2, i2 = lax.top_k(s2, PEER_TOPK)
        cand_s = (v1[..., :, None] + v2[..., None, :]).reshape(PEER_BLOCK, PEER_HEADS, PEER_TOPK * PEER_TOPK)
        cand_i = (i1[..., :, None] * PEER_NKEYS + i2[..., None, :]).reshape(PEER_BLOCK, PEER_HEADS, PEER_TOPK * PEER_TOPK)
        top_s, pos = lax.top_k(cand_s, PEER_TOPK)
        idx = jnp.take_along_axis(cand_i, pos, axis=-1)
        g = jax.nn.softmax(top_s, axis=-1)
        act = jax.nn.gelu(jnp.einsum('thkd,td->thk', expert_u[idx], t).astype(jnp.float32))
        return jnp.einsum('thk,thkd->td', (g * act).astype(t.dtype), expert_v[idx])

    return lax.map(block, tok).reshape(B, L, D)


def setup_inputs(seed: int = 0) -> dict:
    key = jax.random.key(seed)
    ks = jax.random.split(key, 20)
    f32 = jnp.float32
    nrm = lambda k, shape, s: jax.random.normal(k, shape, f32) * s
    return {
        'x': nrm(ks[0], (BATCH, SEQ, D_MODEL), 1.0),
        'c': nrm(ks[1], (BATCH, D_MODEL), 1.0),
        'ctx': nrm(ks[2], (BATCH, CTX_LEN, D_MODEL), 1.0),
        'c_ctx': nrm(ks[3], (D_MODEL,), 1.0),
        'w_ada': nrm(ks[4], (DEPTH, D_MODEL, 6 * D_MODEL), 0.5 * D_MODEL ** -0.5),
        'b_ada': nrm(ks[5], (DEPTH, 6 * D_MODEL), 0.01),
        'w_in': nrm(ks[6], (DEPTH, D_MODEL, D_IN), D_MODEL ** -0.5),
        'w_gk2': nrm(ks[7], (DEPTH, 2, GLA_GATE_RANK, GLA_KW), GLA_GATE_RANK ** -0.5),
        'b_gk': nrm(ks[8], (DEPTH, 2, GLA_KW), 0.01),
        'hg_lower_bounds': nrm(ks[9], (2, DEPTH, HG_KW), 0.1),
        'hg_norm': 1.0 + nrm(ks[10], (DEPTH, HG_DV), 0.01),
        'gla_norm': 1.0 + nrm(ks[11], (DEPTH, GLA_DV), 0.01),
        'w_out': nrm(ks[12], (DEPTH, D_MIX, D_MODEL), BETA * D_MIX ** -0.5),
        'ln_gamma': 1.0 + nrm(ks[13], (DEPTH, 2, D_MODEL), 0.01),
        'ln_beta': nrm(ks[14], (DEPTH, 2, D_MODEL), 0.01),
        'peer_w_query': nrm(ks[15], (DEPTH, D_MODEL, PEER_HEADS * PEER_DQ), D_MODEL ** -0.5),
        'peer_sub_keys': nrm(ks[16], (DEPTH, 2, PEER_NKEYS, PEER_DQ // 2), (PEER_DQ // 2) ** -0.5),
        'peer_u': nrm(ks[17], (DEPTH, PEER_EXPERTS, D_MODEL), D_MODEL ** -0.5),
        'peer_v': nrm(ks[18], (DEPTH, PEER_EXPERTS, D_MODEL), BETA),
    }


def reference(x, c, ctx, c_ctx, w_ada, b_ada, w_in, w_gk2, b_gk, hg_lower_bounds, hg_norm, gla_norm,
              w_out, ln_gamma, ln_beta, peer_w_query, peer_sub_keys, peer_u, peer_v):
    rows = x.shape[1] // GRID_W
    sm = jax.nn.softmax(hg_lower_bounds.astype(jnp.float32), axis=1)
    lb_all = jnp.clip(jnp.cumsum(sm, axis=1) - sm[:, :1], 0.0, 1.0 - 1e-6)
    xc = ctx
    for i in range(DEPTH):
        last = i == DEPTH - 1
        m_l = jax.nn.silu(c) @ w_ada[i] + b_ada[i]
        m_c = jax.nn.silu(c_ctx) @ w_ada[i] + b_ada[i]
        sh1_l, sc1_l, g1_l, sh2_l, sc2_l, g2_l = [t[:, None, :] for t in jnp.split(m_l, 6, axis=-1)]
        sh1_c, sc1_c, g1_c, sh2_c, sc2_c, g2_c = jnp.split(m_c, 6, axis=-1)
        u_l = modulate(x, sh1_l, sc1_l) @ w_in[i]
        u_c = modulate(xc, sh1_c, sc1_c) @ w_in[i]
        mix_c, mix_l = token_mixer(u_c, u_l, rows, lb_all[0, i], lb_all[1, i], w_gk2[i], b_gk[i],
                                   hg_norm[i], gla_norm[i])
        x = layer_norm(ALPHA * x + g1_l * (mix_l @ w_out[i]), ln_gamma[i, 0], ln_beta[i, 0])
        if not last:
            xc = layer_norm(ALPHA * xc + g1_c * (mix_c @ w_out[i]), ln_gamma[i, 0], ln_beta[i, 0])
        y_l = peer(modulate(x, sh2_l, sc2_l), peer_w_query[i], peer_sub_keys[i], peer_u[i], peer_v[i])
        x = layer_norm(ALPHA * x + g2_l * y_l, ln_gamma[i, 1], ln_beta[i, 1])
        if not last:
            y_c = peer(modulate(xc, sh2_c, sc2_c), peer_w_query[i], peer_sub_keys[i], peer_u[i], peer_v[i])
            xc = layer_norm(ALPHA * xc + g2_c * y_c, ln_gamma[i, 1], ln_beta[i, 1])
    return x
```

```python
import functools

import jax
import jax.numpy as jnp
from jax import lax
from jax.experimental import pallas as pl
from jax.experimental.pallas import tpu as pltpu

F32 = jnp.float32
BF16 = jnp.bfloat16

GRID_W = 64
HG_HEADS = 4
HG_DK = 128
GLA_HEADS = 4
GLA_DK = 64
GLA_DV = 128
GLA_GATE_RANK = 16
GLA_GATE_NORM = 16.0
PEER_HEADS = 8
PEER_NKEYS = 128
PEER_TOPK = 16
EPS = 1e-6
LB_FLOOR = 1e-30

LANES = 128
SUBLANES = 8
VMEM_LIMIT = 56 * 1024 * 1024

_NT = (((1,), (1,)), ((), ()))
_TN = (((0,), (0,)), ((), ()))


def _cparams(sem):
    return pltpu.CompilerParams(dimension_semantics=sem, vmem_limit_bytes=VMEM_LIMIT)


def _layer_norm(x):
    mu = jnp.mean(x, axis=-1, keepdims=True)
    xc = x - mu
    var = jnp.mean(xc * xc, axis=-1, keepdims=True)
    return xc * lax.rsqrt(var + EPS)


def _silu(x):
    return x * jax.nn.sigmoid(x)


def _ada_kernel(c_ref, w_ref, b_ref, o_ref):
    s = _silu(c_ref[...]).astype(BF16)
    o_ref[0] = jnp.dot(s, w_ref[0].astype(BF16), preferred_element_type=F32) + b_ref[0]


def ada_mods(cs, w_ada, b_ada):
    depth, d, n = w_ada.shape
    r = cs.shape[0]
    tn = n // 4
    return pl.pallas_call(
        _ada_kernel,
        grid=(depth, n // tn),
        in_specs=[pl.BlockSpec((r, d), lambda l, j: (0, 0)),
                  pl.BlockSpec((1, d, tn), lambda l, j: (l, 0, j)),
                  pl.BlockSpec((1, 1, tn), lambda l, j: (l, 0, j))],
        out_specs=pl.BlockSpec((1, r, tn), lambda l, j: (l, 0, j)),
        out_shape=jax.ShapeDtypeStruct((depth, r, n), F32),
        compiler_params=_cparams(("parallel", "parallel")),
    )(cs, w_ada, b_ada.reshape(depth, 1, n))


def _ln_mod_matmul_kernel(x_ref, sh_ref, sc_ref, w_ref, o_ref):
    y = _layer_norm(x_ref[0]) * (1.0 + sc_ref[0]) + sh_ref[0]
    o_ref[0] = jnp.dot(y.astype(BF16), w_ref[...], preferred_element_type=F32)


def ln_mod_matmul(x, shift, scale, w_bf16, tile):
    b, l, d = x.shape
    n = w_bf16.shape[1]
    return pl.pallas_call(
        _ln_mod_matmul_kernel,
        grid=(b, l // tile),
        in_specs=[pl.BlockSpec((1, tile, d), lambda i, j: (i, j, 0)),
                  pl.BlockSpec((1, 1, d), lambda i, j: (i, 0, 0)),
                  pl.BlockSpec((1, 1, d), lambda i, j: (i, 0, 0)),
                  pl.BlockSpec((d, n), lambda i, j: (0, 0))],
        out_specs=pl.BlockSpec((1, tile, n), lambda i, j: (i, j, 0)),
        out_shape=jax.ShapeDtypeStruct((b, l, n), F32),
        compiler_params=_cparams(("parallel", "parallel")),
    )(x, shift, scale, w_bf16)


def _chunk_scan(q, k, v, la, st_ref, reverse):
    c, dk = q.shape
    nlev = c.bit_length() - 1
    row = lax.broadcasted_iota(jnp.int32, (c, dk), 0)
    qi = lax.broadcasted_iota(jnp.int32, (c, c), 0)
    ki = lax.broadcasted_iota(jnp.int32, (c, c), 1)
    late, early = (ki, qi) if reverse else (qi, ki)
    scores = jnp.where(qi == ki,
                       lax.dot_general(q.astype(BF16), k.astype(BF16), _NT, preferred_element_type=F32), 0.0)
    p = la
    t = la
    for lev in range(nlev):
        m = 1 << lev
        right = (row & m) != 0
        if reverse:
            d = jnp.where(right, p - la, t - p + la)
        else:
            d = jnp.where(right, p, t - p)
        e = jnp.exp(d)
        s_l = lax.dot_general((q * e).astype(BF16), (k * e).astype(BF16), _NT, preferred_element_type=F32)
        lb = late >> lev
        mask = (lb == (early >> lev) + 1) & ((lb & 1) == 1)
        scores = jnp.where(mask, s_l, scores)
        sib = jnp.where(right, pltpu.roll(t, m, 0), pltpu.roll(t, c - m, 0))
        p = p + jnp.where(right, sib, 0.0)
        t = t + sib
    if reverse:
        dq, dkk = t - p + la, p - la
    else:
        dq, dkk = p, t - p
    st = st_ref[...]
    qs = (q * jnp.exp(dq)).astype(BF16)
    ks = (k * jnp.exp(dkk)).astype(BF16)
    vb = v.astype(BF16)
    o = (jnp.dot(scores.astype(BF16), vb, preferred_element_type=F32)
         + lax.dot_general(qs, st.astype(BF16), _NT, preferred_element_type=F32))
    st_ref[...] = st * jnp.exp(t[0:1, :]) + lax.dot_general(vb, ks, _TN, preferred_element_type=F32)
    return o


def _hgrn_gate(z, c0, c1, om):
    ls = jnp.minimum(z, 0.0) - jnp.log1p(jnp.exp(-jnp.abs(z)))
    y = c1 + ls
    la = jnp.maximum(c0, y) + jnp.log1p(jnp.exp(-jnp.abs(c0 - y)))
    return om * jax.nn.sigmoid(-z), la


def _scan_prologue(s0_ref, st_ref):
    @pl.when(pl.program_id(2) == 0)
    def _():
        st_ref[...] = s0_ref[0, 0]


def _scan_epilogue(sfin_ref, st_ref):
    @pl.when(pl.program_id(2) == pl.num_programs(2) - 1)
    def _():
        sfin_ref[0, 0] = st_ref[...]


def _hgrn_scan_kernel(reverse, q_ref, v_ref, z_ref, cst_ref, s0_ref, o_ref, sfin_ref, st_ref):
    _scan_prologue(s0_ref, st_ref)
    k, la = _hgrn_gate(z_ref[0], cst_ref[0:1, :], cst_ref[1:2, :], cst_ref[2:3, :])
    o_ref[0] = _chunk_scan(q_ref[0], k, v_ref[0], la, st_ref, reverse)
    _scan_epilogue(sfin_ref, st_ref)


def _gla_scan_kernel(reverse, q_ref, k_ref, v_ref, r_ref, w_ref, b_ref, s0_ref, o_ref, sfin_ref, st_ref):
    _scan_prologue(s0_ref, st_ref)
    g = jnp.dot(r_ref[0].astype(BF16), w_ref[...].astype(BF16), preferred_element_type=F32) + b_ref[...]
    la = (jnp.minimum(g, 0.0) - jnp.log1p(jnp.exp(-jnp.abs(g)))) * (1.0 / GLA_GATE_NORM)
    q = q_ref[0] * (GLA_DK ** -0.5)
    o_ref[0] = _chunk_scan(q, k_ref[0], v_ref[0], la, st_ref, reverse)
    _scan_epilogue(sfin_ref, st_ref)


def _scan_chunk(l):
    return 128 if l % 128 == 0 else 64


def _scan_call(kernel, reverse, batch, l, heads, in_arrays, in_specs, s0):
    c = _scan_chunk(l)
    n = l // c
    state_spec = pl.BlockSpec((1, 1, LANES, LANES), lambda b, h, j: (b, h, 0, 0))
    return pl.pallas_call(
        functools.partial(kernel, reverse),
        grid=(batch, heads, n),
        in_specs=in_specs + [state_spec],
        out_specs=[pl.BlockSpec((1, c, LANES), lambda b, h, j: (b, (n - 1 - j) if reverse else j, h)),
                   state_spec],
        out_shape=[jax.ShapeDtypeStruct((batch, l, heads * LANES), F32),
                   jax.ShapeDtypeStruct((batch, heads, LANES, LANES), F32)],
        scratch_shapes=[pltpu.VMEM((LANES, LANES), F32)],
        compiler_params=_cparams(("parallel", "parallel", "arbitrary")),
    )(*in_arrays, s0)


def hgrn_scan(u, gate_block, consts, s0, reverse):
    b, l, _ = u.shape
    c = _scan_chunk(l)
    n = l // c

    def col(block0):
        return pl.BlockSpec((1, c, LANES), lambda i, h, j: (i, (n - 1 - j) if reverse else j, block0 + h))

    in_specs = [col(0), col(3 * HG_HEADS), col(gate_block),
                pl.BlockSpec((3, LANES), lambda i, h, j: (0, h))]
    return _scan_call(_hgrn_scan_kernel, reverse, b, l, HG_HEADS, (u, u, u, consts), in_specs, s0)


def gla_scan(q, k, v, r, w, bias, s0, reverse):
    b, l, _ = q.shape
    c = _scan_chunk(l)
    n = l // c
    rank = r.shape[-1]

    def col():
        return pl.BlockSpec((1, c, LANES), lambda i, h, j: (i, (n - 1 - j) if reverse else j, h))

    in_specs = [col(), col(), col(),
                pl.BlockSpec((1, c, rank), lambda i, h, j: (i, (n - 1 - j) if reverse else j, 0)),
                pl.BlockSpec((rank, LANES), lambda i, h, j: (0, h)),
                pl.BlockSpec((1, LANES), lambda i, h, j: (0, h))]
    return _scan_call(_gla_scan_kernel, reverse, b, l, GLA_HEADS, (q, k, v, r, w, bias), in_specs, s0)


def _head_rms(a, gain, heads):
    outs = []
    for h in range(heads):
        ah = a[:, h * LANES:(h + 1) * LANES]
        outs.append(ah * lax.rsqrt(jnp.mean(ah * ah, axis=-1, keepdims=True) + EPS) * gain)
    return jnp.concatenate(outs, axis=1)


def _mix_out_kernel(alpha, hf_ref, hb_ref, gf_ref, gb_ref, hgate_ref, ggate_ref, x_ref, g1_ref,
                    hgain_ref, ggain_ref, w_ref, gam_ref, bet_ref, o_ref):
    ma = _head_rms(hf_ref[0] + hb_ref[0], hgain_ref[...], HG_HEADS) * _silu(hgate_ref[0])
    mb = _head_rms(gf_ref[0] + gb_ref[0], ggain_ref[...], GLA_HEADS) * _silu(ggate_ref[0])
    mix = jnp.concatenate([ma, mb], axis=1).astype(BF16)
    y = jnp.dot(mix, w_ref[...], preferred_element_type=F32)
    r = alpha * x_ref[0] + g1_ref[0] * y
    o_ref[0] = _layer_norm(r) * gam_ref[...] + bet_ref[...]


def mix_out(alpha, hf, hb, gf, gb, u, x, g1, hgain, ggain, w_out_bf16, gamma, beta, tile):
    b, l, d = x.shape
    hw = HG_HEADS * LANES
    tok = lambda width, blk: pl.BlockSpec((1, tile, width), lambda i, j: (i, j, blk))
    vec = lambda width: pl.BlockSpec((1, width), lambda i, j: (0, 0))
    return pl.pallas_call(
        functools.partial(_mix_out_kernel, alpha),
        grid=(b, l // tile),
        in_specs=[tok(hw, 0), tok(hw, 0), tok(hw, 0), tok(hw, 0),
                  tok(hw, 4), tok(hw, 7), tok(d, 0),
                  pl.BlockSpec((1, 1, d), lambda i, j: (i, 0, 0)),
                  vec(LANES), vec(LANES),
                  pl.BlockSpec(w_out_bf16.shape, lambda i, j: (0, 0)),
                  vec(d), vec(d)],
        out_specs=tok(d, 0),
        out_shape=jax.ShapeDtypeStruct((b, l, d), F32),
        compiler_params=_cparams(("parallel", "parallel")),
    )(hf, hb, gf, gb, u, u, x, g1, hgain, ggain, w_out_bf16, gamma, beta)


def _topk_rows(s, iota, nrows, vals_ref, idx_ref, payload=None):
    def body(j, s):
        m = jnp.max(s, axis=0, keepdims=True)
        pos = jnp.min(jnp.where(s == m, iota, nrows), axis=0, keepdims=True)
        hit = iota == pos
        vals_ref[pl.ds(j, 1), :] = m
        if payload is None:
            idx_ref[pl.ds(j, 1), :] = pos
        else:
            idx_ref[pl.ds(j, 1), :] = jnp.max(jnp.where(hit, payload, -1), axis=0, keepdims=True)
        return jnp.where(hit, -jnp.inf, s)
    lax.fori_loop(0, PEER_TOPK, body, s)


def _peer_route_kernel(q_ref, k1_ref, k2_ref, idx_ref, g_ref, v1, i1, v2, i2, cs, ci, tv):
    tt = q_ref.shape[1]
    half = PEER_NKEYS
    q = q_ref[0]
    iota = lax.broadcasted_iota(jnp.int32, (PEER_NKEYS, tt), 0)
    s1 = lax.dot_general(k1_ref[...], q[:, :half], _NT, precision=lax.Precision.HIGHEST,
                         preferred_element_type=F32)
    _topk_rows(s1, iota, PEER_NKEYS, v1, i1)
    s2 = lax.dot_general(k2_ref[...], q[:, half:], _NT, precision=lax.Precision.HIGHEST,
                         preferred_element_type=F32)
    _topk_rows(s2, iota, PEER_NKEYS, v2, i2)
    for a in range(PEER_TOPK):
        cs[a * PEER_TOPK:(a + 1) * PEER_TOPK, :] = v1[a:a + 1, :] + v2[...]
        ci[a * PEER_TOPK:(a + 1) * PEER_TOPK, :] = i1[a:a + 1, :] * PEER_NKEYS + i2[...]
    ncand = PEER_TOPK * PEER_TOPK
    iota2 = lax.broadcasted_iota(jnp.int32, (ncand, tt), 0)
    _topk_rows(cs[...], iota2, ncand, tv, idx_ref.at[0], payload=ci[...])
    top = tv[...]
    e = jnp.exp(top - top[0:1, :])
    g_ref[0] = e / jnp.sum(e, axis=0, keepdims=True)


def peer_route(q, sub_keys, tile):
    b, l, _ = q.shape
    dq = 2 * PEER_NKEYS
    out_spec = pl.BlockSpec((1, PEER_TOPK, tile), lambda i, j, h: (i, h, j))
    nslot = PEER_HEADS * PEER_TOPK
    return pl.pallas_call(
        _peer_route_kernel,
        grid=(b, l // tile, PEER_HEADS),
        in_specs=[pl.BlockSpec((1, tile, dq), lambda i, j, h: (i, j, h)),
                  pl.BlockSpec((PEER_NKEYS, PEER_NKEYS), lambda i, j, h: (0, 0)),
                  pl.BlockSpec((PEER_NKEYS, PEER_NKEYS), lambda i, j, h: (0, 0))],
        out_specs=[out_spec, out_spec],
        out_shape=[jax.ShapeDtypeStruct((b, nslot, l), jnp.int32),
                   jax.ShapeDtypeStruct((b, nslot, l), F32)],
        scratch_shapes=[pltpu.VMEM((PEER_TOPK, tile), F32), pltpu.VMEM((PEER_TOPK, tile), jnp.int32),
                        pltpu.VMEM((PEER_TOPK, tile), F32), pltpu.VMEM((PEER_TOPK, tile), jnp.int32),
                        pltpu.VMEM((PEER_TOPK * PEER_TOPK, tile), F32),
                        pltpu.VMEM((PEER_TOPK * PEER_TOPK, tile), jnp.int32),
                        pltpu.VMEM((PEER_TOPK, tile), F32)],
        compiler_params=_cparams(("parallel", "parallel", "arbitrary")),
    )(q, sub_keys[0], sub_keys[1])


PEER_TOK_TILE = 8
ROW_TILES = 2 * SUBLANES


def _peer_experts_kernel(alpha, idx_ref, idx_next_ref, g_ref, x_ref, sh_ref, sc_ref, g2_ref, gam_ref, bet_ref,
                         uv_hbm, o_ref, buf, y_ref, sem):
    tt = x_ref.shape[0]
    d = x_ref.shape[1]
    nslot = g_ref.shape[1]
    nchunk = d // LANES
    step = pl.program_id(0)
    nsteps = pl.num_programs(0)
    slot = step % 2

    def issue(ids, dst_slot):
        def per_token(t, carry):
            def per_group(g, carry):
                for j in range(SUBLANES):
                    k = g * SUBLANES + j
                    row = pl.multiple_of((t * nslot + k) * ROW_TILES, ROW_TILES)
                    pltpu.make_async_copy(uv_hbm.at[ids[0, t, k]], buf.at[dst_slot, pl.ds(row, ROW_TILES)],
                                          sem.at[dst_slot]).start()
                return carry
            return lax.fori_loop(0, nslot // SUBLANES, per_group, carry)
        lax.fori_loop(0, tt, per_token, 0)

    @pl.when(step == 0)
    def _():
        issue(idx_ref, 0)

    @pl.when(step + 1 < nsteps)
    def _():
        issue(idx_next_ref, 1 - slot)

    x = x_ref[...]
    hm = _layer_norm(x) * (1.0 + sc_ref[0]) + sh_ref[0]

    pltpu.make_async_copy(buf.at[slot], buf.at[slot], sem.at[slot]).wait()

    for t in range(tt):
        base = t * nslot * ROW_TILES
        acc = jnp.zeros((nslot, LANES), F32)
        for c in range(nchunk):
            uc = buf[slot, pl.ds(base + c, nslot, stride=ROW_TILES), :]
            acc = acc + uc * hm[t:t + 1, c * LANES:(c + 1) * LANES]
        a = jnp.sum(acc, axis=1, keepdims=True)
        w = jax.nn.gelu(a) * g_ref[0, :, t:t + 1]
        for c in range(nchunk):
            vc = buf[slot, pl.ds(base + nchunk + c, nslot, stride=ROW_TILES), :]
            y_ref[t:t + 1, c * LANES:(c + 1) * LANES] = jnp.sum(vc * w, axis=0, keepdims=True)

    r = alpha * x + g2_ref[0] * y_ref[...]
    o_ref[...] = _layer_norm(r) * gam_ref[...] + bet_ref[...]


def peer_experts(alpha, x, shift, scale, g2, gamma, beta, idx_t, gates_t, uv):
    b, l, d = x.shape
    assert d == SUBLANES * LANES
    nslot = idx_t.shape[1]
    tt = PEER_TOK_TILE
    tiles_per_batch = l // tt
    nt = b * tiles_per_batch
    idx = idx_t.transpose(0, 2, 1).reshape(nt, tt, nslot)
    gates = gates_t.reshape(b, nslot, tiles_per_batch, tt).transpose(0, 2, 1, 3).reshape(nt, nslot, tt)
    per_batch = lambda i: (i // tiles_per_batch, 0, 0)
    vec = pl.BlockSpec((1, d), lambda i: (0, 0))
    out = pl.pallas_call(
        functools.partial(_peer_experts_kernel, alpha),
        grid=(nt,),
        in_specs=[pl.BlockSpec((1, tt, nslot), lambda i: (i, 0, 0), memory_space=pltpu.SMEM),
                  pl.BlockSpec((1, tt, nslot), lambda i: (jnp.minimum(i + 1, nt - 1), 0, 0),
                               memory_space=pltpu.SMEM),
                  pl.BlockSpec((1, nslot, tt), lambda i: (i, 0, 0)),
                  pl.BlockSpec((tt, d), lambda i: (i, 0)),
                  pl.BlockSpec((1, 1, d), per_batch),
                  pl.BlockSpec((1, 1, d), per_batch),
                  pl.BlockSpec((1, 1, d), per_batch),
                  vec, vec,
                  pl.BlockSpec(memory_space=pl.ANY)],
        out_specs=pl.BlockSpec((tt, d), lambda i: (i, 0)),
        out_shape=jax.ShapeDtypeStruct((b * l, d), F32),
        scratch_shapes=[pltpu.VMEM((2, tt * nslot * ROW_TILES, LANES), F32),
                        pltpu.VMEM((tt, d), F32),
                        pltpu.SemaphoreType.DMA((2,))],
        compiler_params=_cparams(("arbitrary",)),
    )(idx, idx, gates, x.reshape(b * l, d), shift, scale, g2, gamma, beta, uv)
    return out.reshape(b, l, d)


def _to_col_major(t, rows):
    b, l, c = t.shape
    return t.reshape(b, rows, GRID_W, c).transpose(0, 2, 1, 3).reshape(b, l, c)


def _from_col_major(t, rows):
    b, l, c = t.shape
    return t.reshape(b, GRID_W, rows, c).transpose(0, 2, 1, 3).reshape(b, l, c)


def _pad_heads(t):
    b, l, _ = t.shape
    t = t.reshape(b, l, GLA_HEADS, GLA_DK)
    t = jnp.pad(t, ((0, 0), (0, 0), (0, 0), (0, LANES - GLA_DK)))
    return t.reshape(b, l, GLA_HEADS * LANES)


def _gla_parts(u, rows):
    o = 5 * HG_HEADS * LANES
    kw = GLA_HEADS * GLA_DK
    vw = GLA_HEADS * GLA_DV
    order = (lambda t: t) if rows is None else (lambda t: _to_col_major(t, rows))
    q = order(_pad_heads(u[..., o:o + kw]))
    k = order(_pad_heads(u[..., o + kw:o + 2 * kw]))
    v = order(u[..., o + 2 * kw:o + 2 * kw + vw])
    r0 = o + 2 * kw + 2 * vw
    rf = order(u[..., r0:r0 + GLA_GATE_RANK])
    rb = order(u[..., r0 + GLA_GATE_RANK:r0 + 2 * GLA_GATE_RANK])
    return q, k, v, rf, rb


def _token_tile(l, want):
    while l % want:
        want //= 2
    return want


def kernel(x, c, ctx, c_ctx, w_ada, b_ada, w_in, w_gk2, b_gk, hg_lower_bounds, hg_norm, gla_norm, w_out,
           ln_gamma, ln_beta, peer_w_query, peer_sub_keys, peer_u, peer_v):
    batch, seq, d = x.shape
    depth = w_ada.shape[0]
    rows = seq // GRID_W
    alpha = (2.0 * depth) ** 0.25
    n_exp = peer_u.shape[1]

    sm = jax.nn.softmax(hg_lower_bounds.astype(F32), axis=1)
    lb_all = jnp.clip(jnp.cumsum(sm, axis=1) - sm[:, :1], 0.0, 1.0 - 1e-6)
    hg_consts = jnp.stack([jnp.log(jnp.maximum(lb_all, LB_FLOOR)), jnp.log1p(-lb_all), 1.0 - lb_all], axis=2)

    cs = jnp.concatenate([c, c_ctx[None, :], jnp.zeros((SUBLANES - 1 - batch % SUBLANES, d), F32)], axis=0)
    mods = ada_mods(cs, w_ada, b_ada)

    d_in = w_in.shape[2]
    d_in_pad = -(-d_in // LANES) * LANES
    zero_state = jnp.zeros((batch, HG_HEADS, LANES, LANES), F32)
    lat_tile = _token_tile(seq, 256)
    ctx_tile = _token_tile(ctx.shape[1], 256)
    gw_pad = lambda w: _pad_heads(w[None])[0]

    xc = ctx
    for i in range(depth):
        last = i == depth - 1
        m_l = mods[i, :batch].reshape(batch, 1, 6, d)
        m_c = jnp.broadcast_to(mods[i, batch].reshape(1, 1, 6, d), (batch, 1, 6, d))
        w_in_b = jnp.pad(w_in[i], ((0, 0), (0, d_in_pad - d_in))).astype(BF16)
        w_out_b = w_out[i].astype(BF16)
        w_q_b = peer_w_query[i].astype(BF16)
        gam, bet = ln_gamma[i], ln_beta[i]
        hgain, ggain = hg_norm[i][None, :], gla_norm[i][None, :]
        gk_w = [gw_pad(w_gk2[i, dd]) for dd in range(2)]
        gk_b = [gw_pad(b_gk[i, dd][None, :]) for dd in range(2)]
        uv = jnp.concatenate([peer_u[i].reshape(n_exp, SUBLANES, LANES),
                              peer_v[i].reshape(n_exp, SUBLANES, LANES)], axis=1)

        u_c = ln_mod_matmul(xc, m_c[:, :, 0], m_c[:, :, 1], w_in_b, ctx_tile)
        u_l = ln_mod_matmul(x, m_l[:, :, 0], m_l[:, :, 1], w_in_b, lat_tile)
        h_out, g_out = {}, {}
        gc = _gla_parts(u_c, None)
        gl = _gla_parts(u_l, rows)
        for dd, rev in ((0, False), (1, True)):
            o_c, s_c = hgrn_scan(u_c, (1 + dd) * HG_HEADS, hg_consts[dd, i], zero_state, rev)
            o_l, _ = hgrn_scan(u_l, (1 + dd) * HG_HEADS, hg_consts[dd, i], s_c, rev)
            h_out[dd] = (o_c, o_l)
            o_c, s_c = gla_scan(gc[0], gc[1], gc[2], gc[3 + dd], gk_w[dd], gk_b[dd], zero_state, rev)
            o_l, _ = gla_scan(gl[0], gl[1], gl[2], gl[3 + dd], gk_w[dd], gk_b[dd], s_c, rev)
            g_out[dd] = (o_c, _from_col_major(o_l, rows))
        x = mix_out(alpha, h_out[0][1], h_out[1][1], g_out[0][1], g_out[1][1], u_l, x, m_l[:, :, 2],
                    hgain, ggain, w_out_b, gam[0:1], bet[0:1], lat_tile)
        if not last:
            xc = mix_out(alpha, h_out[0][0], h_out[1][0], g_out[0][0], g_out[1][0], u_c, xc, m_c[:, :, 2],
                         hgain, ggain, w_out_b, gam[0:1], bet[0:1], ctx_tile)

        def peer_block(h, m, tile):
            qh = ln_mod_matmul(h, m[:, :, 3], m[:, :, 4], w_q_b, tile)
            idx_t, gates_t = peer_route(qh, peer_sub_keys[i], _token_tile(h.shape[1], 256))
            return peer_experts(alpha, h, m[:, :, 3], m[:, :, 4], m[:, :, 5], gam[1:2], bet[1:2], idx_t, gates_t, uv)

        x = peer_block(x, m_l, lat_tile)
        if not last:
            xc = peer_block(xc, m_c, ctx_tile)
    return x
```

```python
import functools

import jax
import jax.numpy as jnp
from jax import lax
from jax.experimental import pallas as pl
from jax.experimental.pallas import tpu as pltpu

F32 = jnp.float32
BF16 = jnp.bfloat16

GRID_W = 64
HG_HEADS = 4
HG_DK = 128
GLA_HEADS = 4
GLA_DK = 64
GLA_DV = 128
GLA_GATE_RANK = 16
GLA_GATE_NORM = 16.0
PEER_HEADS = 8
PEER_NKEYS = 128
PEER_TOPK = 16
EPS = 1e-6
LB_FLOOR = 1e-30

LANES = 128
SUBLANES = 8
VMEM_LIMIT = 56 * 1024 * 1024

_NT = (((1,), (1,)), ((), ()))
_TN = (((0,), (0,)), ((), ()))


def _cparams(sem):
    return pltpu.CompilerParams(dimension_semantics=sem, vmem_limit_bytes=VMEM_LIMIT)


def _layer_norm(x):
    mu = jnp.mean(x, axis=-1, keepdims=True)
    xc = x - mu
    var = jnp.mean(xc * xc, axis=-1, keepdims=True)
    return xc * lax.rsqrt(var + EPS)


def _silu(x):
    return x * jax.nn.sigmoid(x)


def _ada_kernel(c_ref, w_ref, b_ref, o_ref):
    s = _silu(c_ref[...]).astype(BF16)
    o_ref[0] = jnp.dot(s, w_ref[0].astype(BF16), preferred_element_type=F32) + b_ref[0]


def ada_mods(cs, w_ada, b_ada):
    depth, d, n = w_ada.shape
    r = cs.shape[0]
    tn = n // 4
    return pl.pallas_call(
        _ada_kernel,
        grid=(depth, n // tn),
        in_specs=[pl.BlockSpec((r, d), lambda l, j: (0, 0)),
                  pl.BlockSpec((1, d, tn), lambda l, j: (l, 0, j)),
                  pl.BlockSpec((1, 1, tn), lambda l, j: (l, 0, j))],
        out_specs=pl.BlockSpec((1, r, tn), lambda l, j: (l, 0, j)),
        out_shape=jax.ShapeDtypeStruct((depth, r, n), F32),
        compiler_params=_cparams(("parallel", "parallel")),
    )(cs, w_ada, b_ada.reshape(depth, 1, n))


def _ln_mod_matmul_kernel(x_ref, sh_ref, sc_ref, w_ref, o_ref):
    y = _layer_norm(x_ref[0]) * (1.0 + sc_ref[0]) + sh_ref[0]
    o_ref[0] = jnp.dot(y.astype(BF16), w_ref[...], preferred_element_type=F32)


def ln_mod_matmul(x, shift, scale, w_bf16, tile):
    b, l, d = x.shape
    n = w_bf16.shape[1]
    return pl.pallas_call(
        _ln_mod_matmul_kernel,
        grid=(b, l // tile),
        in_specs=[pl.BlockSpec((1, tile, d), lambda i, j: (i, j, 0)),
                  pl.BlockSpec((1, 1, d), lambda i, j: (i, 0, 0)),
                  pl.BlockSpec((1, 1, d), lambda i, j: (i, 0, 0)),
                  pl.BlockSpec((d, n), lambda i, j: (0, 0))],
        out_specs=pl.BlockSpec((1, tile, n), lambda i, j: (i, j, 0)),
        out_shape=jax.ShapeDtypeStruct((b, l, n), F32),
        compiler_params=_cparams(("parallel", "parallel")),
    )(x, shift, scale, w_bf16)


def _chunk_scan(q, k, v, la, st_ref, reverse):
    c, dk = q.shape
    nlev = c.bit_length() - 1
    row = lax.broadcasted_iota(jnp.int32, (c, dk), 0)
    qi = lax.broadcasted_iota(jnp.int32, (c, c), 0)
    ki = lax.broadcasted_iota(jnp.int32, (c, c), 1)
    late, early = (ki, qi) if reverse else (qi, ki)
    scores = jnp.where(qi == ki,
                       lax.dot_general(q.astype(BF16), k.astype(BF16), _NT, preferred_element_type=F32), 0.0)
    p = la
    t = la
    for lev in range(nlev):
        m = 1 << lev
        right = (row & m) != 0
        if reverse:
            d = jnp.where(right, p - la, t - p + la)
        else:
            d = jnp.where(right, p, t - p)
        e = jnp.exp(d)
        s_l = lax.dot_general((q * e).astype(BF16), (k * e).astype(BF16), _NT, preferred_element_type=F32)
        lb = late >> lev
        mask = (lb == (early >> lev) + 1) & ((lb & 1) == 1)
        scores = jnp.where(mask, s_l, scores)
        sib = jnp.where(right, pltpu.roll(t, m, 0), pltpu.roll(t, c - m, 0))
        p = p + jnp.where(right, sib, 0.0)
        t = t + sib
    if reverse:
        dq, dkk = t - p + la, p - la
    else:
        dq, dkk = p, t - p
    st = st_ref[...]
    qs = (q * jnp.exp(dq)).astype(BF16)
    ks = (k * jnp.exp(dkk)).astype(BF16)
    vb = v.astype(BF16)
    o = (jnp.dot(scores.astype(BF16), vb, preferred_element_type=F32)
         + lax.dot_general(qs, st.astype(BF16), _NT, preferred_element_type=F32))
    st_ref[...] = st * jnp.exp(t[0:1, :]) + lax.dot_general(vb, ks, _TN, preferred_element_type=F32)
    return o


def _hgrn_gate(z, c0, c1, om):
    ls = jnp.minimum(z, 0.0) - jnp.log1p(jnp.exp(-jnp.abs(z)))
    y = c1 + ls
    la = jnp.maximum(c0, y) + jnp.log1p(jnp.exp(-jnp.abs(c0 - y)))
    return om * jax.nn.sigmoid(-z), la


def _scan_prologue(s0_ref, st_ref):
    @pl.when(pl.program_id(1) == 0)
    def _():
        st_ref[...] = s0_ref[0]


def _scan_epilogue(sfin_ref, st_ref):
    @pl.when(pl.program_id(1) == pl.num_programs(1) - 1)
    def _():
        sfin_ref[0] = st_ref[...]


def _head(ref, h):
    return ref[0, :, h * LANES:(h + 1) * LANES]


def _hgrn_scan_kernel(reverse, q_ref, v_ref, z_ref, cst_ref, s0_ref, o_ref, sfin_ref, st_ref):
    _scan_prologue(s0_ref, st_ref)
    for h in range(HG_HEADS):
        cols = slice(h * LANES, (h + 1) * LANES)
        k, la = _hgrn_gate(_head(z_ref, h), cst_ref[0:1, cols], cst_ref[1:2, cols], cst_ref[2:3, cols])
        o_ref[0, :, cols] = _chunk_scan(_head(q_ref, h), k, _head(v_ref, h), la, st_ref.at[h], reverse)
    _scan_epilogue(sfin_ref, st_ref)


def _gla_scan_kernel(reverse, q_ref, k_ref, v_ref, r_ref, w_ref, b_ref, s0_ref, o_ref, sfin_ref, st_ref):
    _scan_prologue(s0_ref, st_ref)
    g = jnp.dot(r_ref[0].astype(BF16), w_ref[...].astype(BF16), preferred_element_type=F32) + b_ref[...]
    la_all = (jnp.minimum(g, 0.0) - jnp.log1p(jnp.exp(-jnp.abs(g)))) * (1.0 / GLA_GATE_NORM)
    for h in range(GLA_HEADS):
        cols = slice(h * LANES, (h + 1) * LANES)
        q = _head(q_ref, h) * (GLA_DK ** -0.5)
        o_ref[0, :, cols] = _chunk_scan(q, _head(k_ref, h), _head(v_ref, h), la_all[:, cols], st_ref.at[h], reverse)
    _scan_epilogue(sfin_ref, st_ref)


def _scan_chunk(l):
    return 128 if l % 128 == 0 else 64


def _scan_call(kernel, reverse, batch, l, heads, in_arrays, in_specs, s0):
    c = _scan_chunk(l)
    n = l // c
    width = heads * LANES
    state_spec = pl.BlockSpec((1, heads, LANES, LANES), lambda b, j: (b, 0, 0, 0))
    return pl.pallas_call(
        functools.partial(kernel, reverse),
        grid=(batch, n),
        in_specs=in_specs + [state_spec],
        out_specs=[pl.BlockSpec((1, c, width), lambda b, j: (b, (n - 1 - j) if reverse else j, 0)),
                   state_spec],
        out_shape=[jax.ShapeDtypeStruct((batch, l, width), F32),
                   jax.ShapeDtypeStruct((batch, heads, LANES, LANES), F32)],
        scratch_shapes=[pltpu.VMEM((heads, LANES, LANES), F32)],
        compiler_params=_cparams(("parallel", "arbitrary")),
    )(*in_arrays, s0)


def hgrn_scan(u, gate_block, consts, s0, reverse):
    b, l, _ = u.shape
    c = _scan_chunk(l)
    n = l // c
    width = HG_HEADS * LANES

    def col(block):
        return pl.BlockSpec((1, c, width), lambda i, j: (i, (n - 1 - j) if reverse else j, block))

    in_specs = [col(0), col(3), col(gate_block), pl.BlockSpec((3, width), lambda i, j: (0, 0))]
    return _scan_call(_hgrn_scan_kernel, reverse, b, l, HG_HEADS, (u, u, u, consts), in_specs, s0)


def gla_scan(q, k, v, r, w, bias, s0, reverse):
    b, l, width = q.shape
    c = _scan_chunk(l)
    n = l // c
    rank = r.shape[-1]

    def col(wd):
        return pl.BlockSpec((1, c, wd), lambda i, j: (i, (n - 1 - j) if reverse else j, 0))

    in_specs = [col(width), col(width), col(width), col(rank),
                pl.BlockSpec((rank, width), lambda i, j: (0, 0)),
                pl.BlockSpec((1, width), lambda i, j: (0, 0))]
    return _scan_call(_gla_scan_kernel, reverse, b, l, GLA_HEADS, (q, k, v, r, w, bias), in_specs, s0)


def _head_rms(a, gain, heads):
    outs = []
    for h in range(heads):
        ah = a[:, h * LANES:(h + 1) * LANES]
        outs.append(ah * lax.rsqrt(jnp.mean(ah * ah, axis=-1, keepdims=True) + EPS) * gain)
    return jnp.concatenate(outs, axis=1)


def _mix_out_kernel(alpha, hf_ref, hb_ref, gf_ref, gb_ref, hgate_ref, ggate_ref, x_ref, g1_ref,
                    hgain_ref, ggain_ref, w_ref, gam_ref, bet_ref, o_ref):
    ma = _head_rms(hf_ref[0] + hb_ref[0], hgain_ref[...], HG_HEADS) * _silu(hgate_ref[0])
    mb = _head_rms(gf_ref[0] + gb_ref[0], ggain_ref[...], GLA_HEADS) * _silu(ggate_ref[0])
    mix = jnp.concatenate([ma, mb], axis=1).astype(BF16)
    y = jnp.dot(mix, w_ref[...], preferred_element_type=F32)
    r = alpha * x_ref[0] + g1_ref[0] * y
    o_ref[0] = _layer_norm(r) * gam_ref[...] + bet_ref[...]


def mix_out(alpha, hf, hb, gf, gb, u, x, g1, hgain, ggain, w_out_bf16, gamma, beta, tile):
    b, l, d = x.shape
    hw = HG_HEADS * LANES
    tok = lambda width, blk: pl.BlockSpec((1, tile, width), lambda i, j: (i, j, blk))
    vec = lambda width: pl.BlockSpec((1, width), lambda i, j: (0, 0))
    return pl.pallas_call(
        functools.partial(_mix_out_kernel, alpha),
        grid=(b, l // tile),
        in_specs=[tok(hw, 0), tok(hw, 0), tok(hw, 0), tok(hw, 0),
                  tok(hw, 4), tok(hw, 7), tok(d, 0),
                  pl.BlockSpec((1, 1, d), lambda i, j: (i, 0, 0)),
                  vec(LANES), vec(LANES),
                  pl.BlockSpec(w_out_bf16.shape, lambda i, j: (0, 0)),
                  vec(d), vec(d)],
        out_specs=tok(d, 0),
        out_shape=jax.ShapeDtypeStruct((b, l, d), F32),
        compiler_params=_cparams(("parallel", "parallel")),
    )(hf, hb, gf, gb, u, u, x, g1, hgain, ggain, w_out_bf16, gamma, beta)


_CAND_PIECES = ((0, 16),) + tuple((a, 8) for a in range(1, 8))
_CAND_TAIL = 8
_NCAND = sum(n for _, n in _CAND_PIECES) + _CAND_TAIL


def _extract_top(s, iota, nrows):
    m = jnp.max(s, axis=0, keepdims=True)
    pos = jnp.min(jnp.where(s == m, iota, nrows), axis=0, keepdims=True)
    return m, pos, iota == pos


def _peer_route_kernel(q_ref, k1_ref, k2_ref, idx_ref, g_ref, sc, v1, i1, v2, i2, cs, ci, tv):
    tt = q_ref.shape[1]
    nk, topk = PEER_NKEYS, PEER_TOPK
    for h in range(PEER_HEADS):
        o = 2 * nk * h
        sc[2 * h] = lax.dot_general(k1_ref[...], q_ref[0, :, o:o + nk], _NT,
                                    precision=lax.Precision.HIGHEST, preferred_element_type=F32)
        sc[2 * h + 1] = lax.dot_general(k2_ref[...], q_ref[0, :, o + nk:o + 2 * nk], _NT,
                                        precision=lax.Precision.HIGHEST, preferred_element_type=F32)
    iota = lax.broadcasted_iota(jnp.int32, (nk, tt), 0)
    iota2 = lax.broadcasted_iota(jnp.int32, (_NCAND, tt), 0)

    def per_head(h, carry):
        def first_stage(j, st):
            s1, s2 = st
            m1, p1, hit1 = _extract_top(s1, iota, nk)
            m2, p2, hit2 = _extract_top(s2, iota, nk)
            v1[pl.ds(j, 1), :] = m1
            i1[pl.ds(j, 1), :] = p1
            v2[pl.ds(j, 1), :] = m2
            i2[pl.ds(j, 1), :] = p2
            return jnp.where(hit1, -jnp.inf, s1), jnp.where(hit2, -jnp.inf, s2)

        lax.fori_loop(0, topk, first_stage, (sc[2 * h], sc[2 * h + 1]))
        row = 0
        for a, n in _CAND_PIECES:
            cs[row:row + n, :] = v1[a:a + 1, :] + v2[0:n, :]
            ci[row:row + n, :] = i1[a:a + 1, :] * nk + i2[0:n, :]
            row += n
        cs[row:row + _CAND_TAIL, :] = v1[topk - _CAND_TAIL:topk, :] + v2[0:1, :]
        ci[row:row + _CAND_TAIL, :] = i1[topk - _CAND_TAIL:topk, :] * nk + i2[0:1, :]
        out0 = pl.multiple_of(h * topk, topk)

        def second_stage(j, s):
            m, _, hit = _extract_top(s, iota2, _NCAND)
            tv[pl.ds(j, 1), :] = m
            idx_ref[0, pl.ds(out0 + j, 1), :] = jnp.max(jnp.where(hit, ci[...], -1), axis=0, keepdims=True)
            return jnp.where(hit, -jnp.inf, s)

        lax.fori_loop(0, topk, second_stage, cs[...])
        top = tv[...]
        e = jnp.exp(top - top[0:1, :])
        g_ref[0, pl.ds(out0, topk), :] = e / jnp.sum(e, axis=0, keepdims=True)
        return carry

    lax.fori_loop(0, PEER_HEADS, per_head, 0)


def peer_route(q, sub_keys, tile):
    b, l, dq = q.shape
    nslot = PEER_HEADS * PEER_TOPK
    out_spec = pl.BlockSpec((1, nslot, tile), lambda i, j: (i, 0, j))
    keys = pl.BlockSpec((PEER_NKEYS, PEER_NKEYS), lambda i, j: (0, 0))
    return pl.pallas_call(
        _peer_route_kernel,
        grid=(b, l // tile),
        in_specs=[pl.BlockSpec((1, tile, dq), lambda i, j: (i, j, 0)), keys, keys],
        out_specs=[out_spec, out_spec],
        out_shape=[jax.ShapeDtypeStruct((b, nslot, l), jnp.int32),
                   jax.ShapeDtypeStruct((b, nslot, l), F32)],
        scratch_shapes=[pltpu.VMEM((2 * PEER_HEADS, PEER_NKEYS, tile), F32),
                        pltpu.VMEM((PEER_TOPK, tile), F32), pltpu.VMEM((PEER_TOPK, tile), jnp.int32),
                        pltpu.VMEM((PEER_TOPK, tile), F32), pltpu.VMEM((PEER_TOPK, tile), jnp.int32),
                        pltpu.VMEM((_NCAND, tile), F32), pltpu.VMEM((_NCAND, tile), jnp.int32),
                        pltpu.VMEM((PEER_TOPK, tile), F32)],
        compiler_params=_cparams(("parallel", "parallel")),
    )(q, sub_keys[0], sub_keys[1])


PEER_TOK_TILE = 8
ROW_TILES = 2 * SUBLANES


def _peer_experts_kernel(alpha, ids_ref, ids_next_ref, g_ref, x_ref, sh_ref, sc_ref, g2_ref, gam_ref, bet_ref,
                         uv_hbm, o_ref, buf_a, buf_b, y_ref, sem):
    tt = PEER_TOK_TILE
    d = x_ref.shape[1]
    nslot = g_ref.shape[1]
    nchunk = d // LANES
    step = pl.program_id(0)
    last = pl.num_programs(0) - 1
    bufs = (buf_a, buf_b)

    def row_copy(expert, dst, row, s):
        return pltpu.make_async_copy(uv_hbm.at[expert], dst.at[pl.ds(row, ROW_TILES)], sem.at[s])

    @pl.when(step == 0)
    def _():
        def per_token(t, carry):
            def per_group(g, carry):
                for j in range(SUBLANES):
                    k = g * SUBLANES + j
                    row = pl.multiple_of((t * nslot + k) * ROW_TILES, ROW_TILES)
                    row_copy(ids_ref[0, t, k], buf_a, row, 0).start()
                return carry
            return lax.fori_loop(0, nslot // SUBLANES, per_group, carry)
        lax.fori_loop(0, tt, per_token, 0)

    x = x_ref[...]
    hm = _layer_norm(x) * (1.0 + sc_ref[0]) + sh_ref[0]

    for phase in range(2):
        cur, nxt = bufs[phase], bufs[1 - phase]
        pltpu.make_async_copy(cur, cur, sem.at[phase]).wait()
        for t in range(tt):
            for k in range(nslot):
                expert = ids_ref[0, tt + t, k] if phase == 0 else ids_next_ref[0, t, k]
                row_copy(expert, nxt, (t * nslot + k) * ROW_TILES, 1 - phase).start()
            tok = phase * tt + t
            base = t * nslot * ROW_TILES
            acc = jnp.zeros((nslot, LANES), F32)
            for c in range(nchunk):
                uc = cur[pl.ds(base + c, nslot, stride=ROW_TILES), :]
                acc = acc + uc * hm[tok:tok + 1, c * LANES:(c + 1) * LANES]
            a = jnp.sum(acc, axis=1, keepdims=True)
            w = jax.nn.gelu(a) * g_ref[0, :, tok:tok + 1]
            for c in range(nchunk):
                vc = cur[pl.ds(base + nchunk + c, nslot, stride=ROW_TILES), :]
                y_ref[tok:tok + 1, c * LANES:(c + 1) * LANES] = jnp.sum(vc * w, axis=0, keepdims=True)

    r = alpha * x + g2_ref[0] * y_ref[...]
    o_ref[...] = _layer_norm(r) * gam_ref[...] + bet_ref[...]

    @pl.when(step == last)
    def _():
        pltpu.make_async_copy(buf_a, buf_a, sem.at[0]).wait()


def peer_experts(alpha, x, shift, scale, g2, gamma, beta, idx_t, gates_t, uv):
    b, l, d = x.shape
    assert d == SUBLANES * LANES
    nslot = idx_t.shape[1]
    blk = 2 * PEER_TOK_TILE
    blocks_per_batch = l // blk
    nb = b * blocks_per_batch
    idx = idx_t.transpose(0, 2, 1).reshape(nb, blk, nslot)
    gates = gates_t.reshape(b, nslot, blocks_per_batch, blk).transpose(0, 2, 1, 3).reshape(nb, nslot, blk)
    per_batch = lambda i: (i // blocks_per_batch, 0, 0)
    vec = pl.BlockSpec((1, d), lambda i: (0, 0))
    gather_buf = pltpu.VMEM((PEER_TOK_TILE * nslot * ROW_TILES, LANES), F32)
    out = pl.pallas_call(
        functools.partial(_peer_experts_kernel, alpha),
        grid=(nb,),
        in_specs=[pl.BlockSpec((1, blk, nslot), lambda i: (i, 0, 0), memory_space=pltpu.SMEM),
                  pl.BlockSpec((1, blk, nslot), lambda i: (jnp.minimum(i + 1, nb - 1), 0, 0),
                               memory_space=pltpu.SMEM),
                  pl.BlockSpec((1, nslot, blk), lambda i: (i, 0, 0)),
                  pl.BlockSpec((blk, d), lambda i: (i, 0)),
                  pl.BlockSpec((1, 1, d), per_batch),
                  pl.BlockSpec((1, 1, d), per_batch),
                  pl.BlockSpec((1, 1, d), per_batch),
                  vec, vec,
                  pl.BlockSpec(memory_space=pl.ANY)],
        out_specs=pl.BlockSpec((blk, d), lambda i: (i, 0)),
        out_shape=jax.ShapeDtypeStruct((b * l, d), F32),
        scratch_shapes=[gather_buf, gather_buf, pltpu.VMEM((blk, d), F32), pltpu.SemaphoreType.DMA((2,))],
        compiler_params=_cparams(("arbitrary",)),
    )(idx, idx, gates, x.reshape(b * l, d), shift, scale, g2, gamma, beta, uv)
    return out.reshape(b, l, d)


def _to_col_major(t, rows):
    b, l, c = t.shape
    return t.reshape(b, rows, GRID_W, c).transpose(0, 2, 1, 3).reshape(b, l, c)


def _from_col_major(t, rows):
    b, l, c = t.shape
    return t.reshape(b, GRID_W, rows, c).transpose(0, 2, 1, 3).reshape(b, l, c)


def _pad_heads(t):
    b, l, _ = t.shape
    t = t.reshape(b, l, GLA_HEADS, GLA_DK)
    t = jnp.pad(t, ((0, 0), (0, 0), (0, 0), (0, LANES - GLA_DK)))
    return t.reshape(b, l, GLA_HEADS * LANES)


def _gla_parts(u, rows):
    o = 5 * HG_HEADS * LANES
    kw = GLA_HEADS * GLA_DK
    vw = GLA_HEADS * GLA_DV
    order = (lambda t: t) if rows is None else (lambda t: _to_col_major(t, rows))
    q = order(_pad_heads(u[..., o:o + kw]))
    k = order(_pad_heads(u[..., o + kw:o + 2 * kw]))
    v = order(u[..., o + 2 * kw:o + 2 * kw + vw])
    r0 = o + 2 * kw + 2 * vw
    rf = order(u[..., r0:r0 + GLA_GATE_RANK])
    rb = order(u[..., r0 + GLA_GATE_RANK:r0 + 2 * GLA_GATE_RANK])
    return q, k, v, rf, rb


def _token_tile(l, want):
    while l % want:
        want //= 2
    return want


def kernel(x, c, ctx, c_ctx, w_ada, b_ada, w_in, w_gk2, b_gk, hg_lower_bounds, hg_norm, gla_norm, w_out,
           ln_gamma, ln_beta, peer_w_query, peer_sub_keys, peer_u, peer_v):
    batch, seq, d = x.shape
    depth = w_ada.shape[0]
    rows = seq // GRID_W
    alpha = (2.0 * depth) ** 0.25
    n_exp = peer_u.shape[1]

    sm = jax.nn.softmax(hg_lower_bounds.astype(F32), axis=1)
    lb_all = jnp.clip(jnp.cumsum(sm, axis=1) - sm[:, :1], 0.0, 1.0 - 1e-6)
    hg_consts = jnp.stack([jnp.log(jnp.maximum(lb_all, LB_FLOOR)), jnp.log1p(-lb_all), 1.0 - lb_all], axis=2)

    cs = jnp.concatenate([c, c_ctx[None, :], jnp.zeros((SUBLANES - 1 - batch % SUBLANES, d), F32)], axis=0)
    mods = ada_mods(cs, w_ada, b_ada)

    d_in = w_in.shape[2]
    d_in_pad = -(-d_in // LANES) * LANES
    zero_state = jnp.zeros((batch, HG_HEADS, LANES, LANES), F32)
    lat_tile = _token_tile(seq, 256)
    ctx_tile = _token_tile(ctx.shape[1], 256)
    gw_pad = lambda w: _pad_heads(w[None])[0]

    xc = ctx
    for i in range(depth):
        last = i == depth - 1
        m_l = mods[i, :batch].reshape(batch, 1, 6, d)
        m_c = jnp.broadcast_to(mods[i, batch].reshape(1, 1, 6, d), (batch, 1, 6, d))
        w_in_b = jnp.pad(w_in[i], ((0, 0), (0, d_in_pad - d_in))).astype(BF16)
        w_out_b = w_out[i].astype(BF16)
        w_q_b = peer_w_query[i].astype(BF16)
        gam, bet = ln_gamma[i], ln_beta[i]
        hgain, ggain = hg_norm[i][None, :], gla_norm[i][None, :]
        gk_w = [gw_pad(w_gk2[i, dd]) for dd in range(2)]
        gk_b = [gw_pad(b_gk[i, dd][None, :]) for dd in range(2)]
        uv = jnp.concatenate([peer_u[i].reshape(n_exp, SUBLANES, LANES),
                              peer_v[i].reshape(n_exp, SUBLANES, LANES)], axis=1)

        u_c = ln_mod_matmul(xc, m_c[:, :, 0], m_c[:, :, 1], w_in_b, ctx_tile)
        u_l = ln_mod_matmul(x, m_l[:, :, 0], m_l[:, :, 1], w_in_b, lat_tile)
        h_out, g_out = {}, {}
        gc = _gla_parts(u_c, None)
        gl = _gla_parts(u_l, rows)
        for dd, rev in ((0, False), (1, True)):
            o_c, s_c = hgrn_scan(u_c, 1 + dd, hg_consts[dd, i], zero_state, rev)
            o_l, _ = hgrn_scan(u_l, 1 + dd, hg_consts[dd, i], s_c, rev)
            h_out[dd] = (o_c, o_l)
            o_c, s_c = gla_scan(gc[0], gc[1], gc[2], gc[3 + dd], gk_w[dd], gk_b[dd], zero_state, rev)
            o_l, _ = gla_scan(gl[0], gl[1], gl[2], gl[3 + dd], gk_w[dd], gk_b[dd], s_c, rev)
            g_out[dd] = (o_c, _from_col_major(o_l, rows))
        x = mix_out(alpha, h_out[0][1], h_out[1][1], g_out[0][1], g_out[1][1], u_l, x, m_l[:, :, 2],
                    hgain, ggain, w_out_b, gam[0:1], bet[0:1], lat_tile)
        if not last:
            xc = mix_out(alpha, h_out[0][0], h_out[1][0], g_out[0][0], g_out[1][0], u_c, xc, m_c[:, :, 2],
                         hgain, ggain, w_out_b, gam[0:1], bet[0:1], ctx_tile)

        def peer_block(h, m, tile):
            qh = ln_mod_matmul(h, m[:, :, 3], m[:, :, 4], w_q_b, tile)
            idx_t, gates_t = peer_route(qh, peer_sub_keys[i], _token_tile(h.shape[1], LANES))
            return peer_experts(alpha, h, m[:, :, 3], m[:, :, 4], m[:, :, 5], gam[1:2], bet[1:2], idx_t, gates_t, uv)

        x = peer_block(x, m_l, lat_tile)
        if not last:
            xc = peer_block(xc, m_c, ctx_tile)
    return x
```

```python
import functools

import jax
import jax.numpy as jnp
from jax import lax
from jax.experimental import pallas as pl
from jax.experimental.pallas import tpu as pltpu

F32 = jnp.float32
BF16 = jnp.bfloat16

GRID_W = 64
HG_HEADS = 4
HG_DK = 128
GLA_HEADS = 4
GLA_DK = 64
GLA_DV = 128
GLA_GATE_RANK = 16
GLA_GATE_NORM = 16.0
PEER_HEADS = 8
PEER_NKEYS = 128
PEER_TOPK = 16
EPS = 1e-6
LB_FLOOR = 1e-30

LANES = 128
SUBLANES = 8
VMEM_LIMIT = 56 * 1024 * 1024

_NT = (((1,), (1,)), ((), ()))
_TN = (((0,), (0,)), ((), ()))


def _cparams(sem):
    return pltpu.CompilerParams(dimension_semantics=sem, vmem_limit_bytes=VMEM_LIMIT)


def _layer_norm(x):
    mu = jnp.mean(x, axis=-1, keepdims=True)
    xc = x - mu
    var = jnp.mean(xc * xc, axis=-1, keepdims=True)
    return xc * lax.rsqrt(var + EPS)


def _silu(x):
    return x * jax.nn.sigmoid(x)


def _ada_kernel(c_ref, w_ref, b_ref, o_ref):
    s = _silu(c_ref[...]).astype(BF16)
    o_ref[0] = jnp.dot(s, w_ref[0].astype(BF16), preferred_element_type=F32) + b_ref[0]


def ada_mods(cs, w_ada, b_ada):
    depth, d, n = w_ada.shape
    r = cs.shape[0]
    tn = n // 4
    return pl.pallas_call(
        _ada_kernel,
        grid=(depth, n // tn),
        in_specs=[pl.BlockSpec((r, d), lambda l, j: (0, 0)),
                  pl.BlockSpec((1, d, tn), lambda l, j: (l, 0, j)),
                  pl.BlockSpec((1, 1, tn), lambda l, j: (l, 0, j))],
        out_specs=pl.BlockSpec((1, r, tn), lambda l, j: (l, 0, j)),
        out_shape=jax.ShapeDtypeStruct((depth, r, n), F32),
        compiler_params=_cparams(("parallel", "parallel")),
    )(cs, w_ada, b_ada.reshape(depth, 1, n))


def _ln_mod_matmul_kernel(x_ref, sh_ref, sc_ref, w_ref, o_ref):
    y = _layer_norm(x_ref[0]) * (1.0 + sc_ref[0]) + sh_ref[0]
    o_ref[0] = jnp.dot(y.astype(BF16), w_ref[...], preferred_element_type=F32)


def ln_mod_matmul(x, shift, scale, w_bf16, tile):
    b, l, d = x.shape
    n = w_bf16.shape[1]
    return pl.pallas_call(
        _ln_mod_matmul_kernel,
        grid=(b, l // tile),
        in_specs=[pl.BlockSpec((1, tile, d), lambda i, j: (i, j, 0)),
                  pl.BlockSpec((1, 1, d), lambda i, j: (i, 0, 0)),
                  pl.BlockSpec((1, 1, d), lambda i, j: (i, 0, 0)),
                  pl.BlockSpec((d, n), lambda i, j: (0, 0))],
        out_specs=pl.BlockSpec((1, tile, n), lambda i, j: (i, j, 0)),
        out_shape=jax.ShapeDtypeStruct((b, l, n), F32),
        compiler_params=_cparams(("parallel", "parallel")),
    )(x, shift, scale, w_bf16)


def _chunk_scan(q, k, v, la, st_ref, reverse):
    c, dk = q.shape
    nlev = c.bit_length() - 1
    row = lax.broadcasted_iota(jnp.int32, (c, dk), 0)
    qi = lax.broadcasted_iota(jnp.int32, (c, c), 0)
    ki = lax.broadcasted_iota(jnp.int32, (c, c), 1)
    late, early = (ki, qi) if reverse else (qi, ki)
    scores = jnp.where(qi == ki,
                       lax.dot_general(q.astype(BF16), k.astype(BF16), _NT, preferred_element_type=F32), 0.0)
    p = la
    t = la
    for lev in range(nlev):
        m = 1 << lev
        right = (row & m) != 0
        if reverse:
            d = jnp.where(right, p - la, t - p + la)
        else:
            d = jnp.where(right, p, t - p)
        e = jnp.exp(d)
        s_l = lax.dot_general((q * e).astype(BF16), (k * e).astype(BF16), _NT, preferred_element_type=F32)
        lb = late >> lev
        mask = (lb == (early >> lev) + 1) & ((lb & 1) == 1)
        scores = jnp.where(mask, s_l, scores)
        sib = jnp.where(right, pltpu.roll(t, m, 0), pltpu.roll(t, c - m, 0))
        p = p + jnp.where(right, sib, 0.0)
        t = t + sib
    if reverse:
        dq, dkk = t - p + la, p - la
    else:
        dq, dkk = p, t - p
    st = st_ref[...]
    qs = (q * jnp.exp(dq)).astype(BF16)
    ks = (k * jnp.exp(dkk)).astype(BF16)
    vb = v.astype(BF16)
    o = (jnp.dot(scores.astype(BF16), vb, preferred_element_type=F32)
         + lax.dot_general(qs, st.astype(BF16), _NT, preferred_element_type=F32))
    st_ref[...] = st * jnp.exp(t[0:1, :]) + lax.dot_general(vb, ks, _TN, preferred_element_type=F32)
    return o


def _hgrn_gate(z, c0, c1, om):
    ls = jnp.minimum(z, 0.0) - jnp.log1p(jnp.exp(-jnp.abs(z)))
    y = c1 + ls
    la = jnp.maximum(c0, y) + jnp.log1p(jnp.exp(-jnp.abs(c0 - y)))
    return om * jax.nn.sigmoid(-z), la


def _scan_prologue(s0_ref, st_ref):
    @pl.when(pl.program_id(1) == 0)
    def _():
        st_ref[...] = s0_ref[0]


def _scan_epilogue(sfin_ref, st_ref):
    @pl.when(pl.program_id(1) == pl.num_programs(1) - 1)
    def _():
        sfin_ref[0] = st_ref[...]


def _head(ref, h):
    return ref[0, :, h * LANES:(h + 1) * LANES]


def _hgrn_scan_kernel(reverse, q_ref, v_ref, z_ref, cst_ref, s0_ref, o_ref, sfin_ref, st_ref):
    _scan_prologue(s0_ref, st_ref)
    for h in range(HG_HEADS):
        cols = slice(h * LANES, (h + 1) * LANES)
        k, la = _hgrn_gate(_head(z_ref, h), cst_ref[0:1, cols], cst_ref[1:2, cols], cst_ref[2:3, cols])
        o_ref[0, :, cols] = _chunk_scan(_head(q_ref, h), k, _head(v_ref, h), la, st_ref.at[h], reverse)
    _scan_epilogue(sfin_ref, st_ref)


def _gla_scan_kernel(reverse, q_ref, k_ref, v_ref, r_ref, w_ref, b_ref, s0_ref, o_ref, sfin_ref, st_ref):
    _scan_prologue(s0_ref, st_ref)
    g = jnp.dot(r_ref[0].astype(BF16), w_ref[...].astype(BF16), preferred_element_type=F32) + b_ref[...]
    la_all = (jnp.minimum(g, 0.0) - jnp.log1p(jnp.exp(-jnp.abs(g)))) * (1.0 / GLA_GATE_NORM)
    for h in range(GLA_HEADS):
        cols = slice(h * LANES, (h + 1) * LANES)
        q = _head(q_ref, h) * (GLA_DK ** -0.5)
        o_ref[0, :, cols] = _chunk_scan(q, _head(k_ref, h), _head(v_ref, h), la_all[:, cols], st_ref.at[h], reverse)
    _scan_epilogue(sfin_ref, st_ref)


def _scan_chunk(l):
    return 128 if l % 128 == 0 else 64


def _scan_call(kernel, reverse, batch, l, heads, in_arrays, in_specs, s0):
    c = _scan_chunk(l)
    n = l // c
    width = heads * LANES
    state_spec = pl.BlockSpec((1, heads, LANES, LANES), lambda b, j: (b, 0, 0, 0))
    return pl.pallas_call(
        functools.partial(kernel, reverse),
        grid=(batch, n),
        in_specs=in_specs + [state_spec],
        out_specs=[pl.BlockSpec((1, c, width), lambda b, j: (b, (n - 1 - j) if reverse else j, 0)),
                   state_spec],
        out_shape=[jax.ShapeDtypeStruct((batch, l, width), F32),
                   jax.ShapeDtypeStruct((batch, heads, LANES, LANES), F32)],
        scratch_shapes=[pltpu.VMEM((heads, LANES, LANES), F32)],
        compiler_params=_cparams(("parallel", "arbitrary")),
    )(*in_arrays, s0)


def hgrn_scan(u, gate_block, consts, s0, reverse):
    b, l, _ = u.shape
    c = _scan_chunk(l)
    n = l // c
    width = HG_HEADS * LANES

    def col(block):
        return pl.BlockSpec((1, c, width), lambda i, j: (i, (n - 1 - j) if reverse else j, block))

    in_specs = [col(0), col(3), col(gate_block), pl.BlockSpec((3, width), lambda i, j: (0, 0))]
    return _scan_call(_hgrn_scan_kernel, reverse, b, l, HG_HEADS, (u, u, u, consts), in_specs, s0)


def gla_scan(q, k, v, r, w, bias, s0, reverse):
    b, l, width = q.shape
    c = _scan_chunk(l)
    n = l // c
    rank = r.shape[-1]

    def col(wd):
        return pl.BlockSpec((1, c, wd), lambda i, j: (i, (n - 1 - j) if reverse else j, 0))

    in_specs = [col(width), col(width), col(width), col(rank),
                pl.BlockSpec((rank, width), lambda i, j: (0, 0)),
                pl.BlockSpec((1, width), lambda i, j: (0, 0))]
    return _scan_call(_gla_scan_kernel, reverse, b, l, GLA_HEADS, (q, k, v, r, w, bias), in_specs, s0)


def _head_rms(a, gain, heads):
    outs = []
    for h in range(heads):
        ah = a[:, h * LANES:(h + 1) * LANES]
        outs.append(ah * lax.rsqrt(jnp.mean(ah * ah, axis=-1, keepdims=True) + EPS) * gain)
    return jnp.concatenate(outs, axis=1)


def _mix_out_kernel(alpha, hf_ref, hb_ref, gf_ref, gb_ref, hgate_ref, ggate_ref, x_ref, g1_ref,
                    hgain_ref, ggain_ref, w_ref, gam_ref, bet_ref, o_ref):
    ma = _head_rms(hf_ref[0] + hb_ref[0], hgain_ref[...], HG_HEADS) * _silu(hgate_ref[0])
    mb = _head_rms(gf_ref[0] + gb_ref[0], ggain_ref[...], GLA_HEADS) * _silu(ggate_ref[0])
    mix = jnp.concatenate([ma, mb], axis=1).astype(BF16)
    y = jnp.dot(mix, w_ref[...], preferred_element_type=F32)
    r = alpha * x_ref[0] + g1_ref[0] * y
    o_ref[0] = _layer_norm(r) * gam_ref[...] + bet_ref[...]


def mix_out(alpha, hf, hb, gf, gb, u, x, g1, hgain, ggain, w_out_bf16, gamma, beta, tile):
    b, l, d = x.shape
    hw = HG_HEADS * LANES
    tok = lambda width, blk: pl.BlockSpec((1, tile, width), lambda i, j: (i, j, blk))
    vec = lambda width: pl.BlockSpec((1, width), lambda i, j: (0, 0))
    return pl.pallas_call(
        functools.partial(_mix_out_kernel, alpha),
        grid=(b, l // tile),
        in_specs=[tok(hw, 0), tok(hw, 0), tok(hw, 0), tok(hw, 0),
                  tok(hw, 4), tok(hw, 7), tok(d, 0),
                  pl.BlockSpec((1, 1, d), lambda i, j: (i, 0, 0)),
                  vec(LANES), vec(LANES),
                  pl.BlockSpec(w_out_bf16.shape, lambda i, j: (0, 0)),
                  vec(d), vec(d)],
        out_specs=tok(d, 0),
        out_shape=jax.ShapeDtypeStruct((b, l, d), F32),
        compiler_params=_cparams(("parallel", "parallel")),
    )(hf, hb, gf, gb, u, u, x, g1, hgain, ggain, w_out_bf16, gamma, beta)


_CAND_PIECES = ((0, 16),) + tuple((a, 8) for a in range(1, 8))
_CAND_TAIL = 8
_NCAND = sum(n for _, n in _CAND_PIECES) + _CAND_TAIL


def _extract_top(s, iota, nrows):
    m = jnp.max(s, axis=0, keepdims=True)
    pos = jnp.min(jnp.where(s == m, iota, nrows), axis=0, keepdims=True)
    return m, pos, iota == pos


def _peer_route_kernel(q_ref, k1_ref, k2_ref, idx_ref, g_ref, sc, v1, i1, v2, i2, cs, ci, tv):
    tt = q_ref.shape[1]
    nk, topk = PEER_NKEYS, PEER_TOPK
    for h in range(PEER_HEADS):
        o = 2 * nk * h
        sc[2 * h] = lax.dot_general(k1_ref[...], q_ref[0, :, o:o + nk], _NT,
                                    precision=lax.Precision.HIGHEST, preferred_element_type=F32)
        sc[2 * h + 1] = lax.dot_general(k2_ref[...], q_ref[0, :, o + nk:o + 2 * nk], _NT,
                                        precision=lax.Precision.HIGHEST, preferred_element_type=F32)
    iota = lax.broadcasted_iota(jnp.int32, (nk, tt), 0)
    iota2 = lax.broadcasted_iota(jnp.int32, (_NCAND, tt), 0)

    def per_head(h, carry):
        def first_stage(j, st):
            s1, s2 = st
            m1, p1, hit1 = _extract_top(s1, iota, nk)
            m2, p2, hit2 = _extract_top(s2, iota, nk)
            v1[pl.ds(j, 1), :] = m1
            i1[pl.ds(j, 1), :] = p1
            v2[pl.ds(j, 1), :] = m2
            i2[pl.ds(j, 1), :] = p2
            return jnp.where(hit1, -jnp.inf, s1), jnp.where(hit2, -jnp.inf, s2)

        lax.fori_loop(0, topk, first_stage, (sc[2 * h], sc[2 * h + 1]))
        row = 0
        for a, n in _CAND_PIECES:
            cs[row:row + n, :] = v1[a:a + 1, :] + v2[0:n, :]
            ci[row:row + n, :] = i1[a:a + 1, :] * nk + i2[0:n, :]
            row += n
        cs[row:row + _CAND_TAIL, :] = v1[topk - _CAND_TAIL:topk, :] + v2[0:1, :]
        ci[row:row + _CAND_TAIL, :] = i1[topk - _CAND_TAIL:topk, :] * nk + i2[0:1, :]
        out0 = pl.multiple_of(h * topk, topk)

        def second_stage(j, s):
            m, _, hit = _extract_top(s, iota2, _NCAND)
            tv[pl.ds(j, 1), :] = m
            idx_ref[0, pl.ds(out0 + j, 1), :] = jnp.max(jnp.where(hit, ci[...], -1), axis=0, keepdims=True)
            return jnp.where(hit, -jnp.inf, s)

        lax.fori_loop(0, topk, second_stage, cs[...])
        top = tv[...]
        e = jnp.exp(top - top[0:1, :])
        g_ref[0, pl.ds(out0, topk), :] = e / jnp.sum(e, axis=0, keepdims=True)
        return carry

    lax.fori_loop(0, PEER_HEADS, per_head, 0)


def peer_route(q, sub_keys, tile):
    b, l, dq = q.shape
    nslot = PEER_HEADS * PEER_TOPK
    out_spec = pl.BlockSpec((1, nslot, tile), lambda i, j: (i, 0, j))
    keys = pl.BlockSpec((PEER_NKEYS, PEER_NKEYS), lambda i, j: (0, 0))
    return pl.pallas_call(
        _peer_route_kernel,
        grid=(b, l // tile),
        in_specs=[pl.BlockSpec((1, tile, dq), lambda i, j: (i, j, 0)), keys, keys],
        out_specs=[out_spec, out_spec],
        out_shape=[jax.ShapeDtypeStruct((b, nslot, l), jnp.int32),
                   jax.ShapeDtypeStruct((b, nslot, l), F32)],
        scratch_shapes=[pltpu.VMEM((2 * PEER_HEADS, PEER_NKEYS, tile), F32),
                        pltpu.VMEM((PEER_TOPK, tile), F32), pltpu.VMEM((PEER_TOPK, tile), jnp.int32),
                        pltpu.VMEM((PEER_TOPK, tile), F32), pltpu.VMEM((PEER_TOPK, tile), jnp.int32),
                        pltpu.VMEM((_NCAND, tile), F32), pltpu.VMEM((_NCAND, tile), jnp.int32),
                        pltpu.VMEM((PEER_TOPK, tile), F32)],
        compiler_params=_cparams(("parallel", "parallel")),
    )(q, sub_keys[0], sub_keys[1])


PEER_TOK_TILE = 8
PEER_ISSUE_SLACK = 2
ROW_TILES = 2 * SUBLANES
ROW_PITCH = ROW_TILES + SUBLANES


def _peer_experts_kernel(alpha, ids_ref, ids_next_ref, g_ref, x_ref, sh_ref, sc_ref, g2_ref, gam_ref, bet_ref,
                         uv_hbm, o_ref, buf_a, buf_b, y_ref, sem):
    tt = PEER_TOK_TILE
    d = x_ref.shape[1]
    nslot = g_ref.shape[1]
    nchunk = d // LANES
    step = pl.program_id(0)
    last = pl.num_programs(0) - 1
    bufs = (buf_a, buf_b)

    def row_copy(expert, dst, row, s):
        return pltpu.make_async_copy(uv_hbm.at[expert], dst.at[pl.ds(row, ROW_TILES)], sem.at[s])

    def tile_wait(buf, s):
        filled = buf.at[pl.ds(0, tt * nslot * ROW_TILES)]
        pltpu.make_async_copy(filled, filled, sem.at[s]).wait()

    @pl.when(step == 0)
    def _():
        def per_token(t, carry):
            def per_group(g, carry):
                for j in range(SUBLANES):
                    k = g * SUBLANES + j
                    row = pl.multiple_of((t * nslot + k) * ROW_PITCH, SUBLANES)
                    row_copy(ids_ref[0, t, k], buf_a, row, 0).start(priority=j % 2)
                return carry
            return lax.fori_loop(0, nslot // SUBLANES, per_group, carry)
        lax.fori_loop(0, tt, per_token, 0)

    x = x_ref[...]
    hm = _layer_norm(x) * (1.0 + sc_ref[0]) + sh_ref[0]

    issue_tokens = tt - PEER_ISSUE_SLACK
    per_token_issue = -(-tt * nslot // issue_tokens)

    for phase in range(2):
        cur, nxt = bufs[phase], bufs[1 - phase]
        tile_wait(cur, phase)
        for t in range(tt):
            for n in range(t * per_token_issue, min((t + 1) * per_token_issue, tt * nslot)):
                tn, k = divmod(n, nslot)
                expert = ids_ref[0, tt + tn, k] if phase == 0 else ids_next_ref[0, tn, k]
                row_copy(expert, nxt, n * ROW_PITCH, 1 - phase).start(priority=n % 2)
            tok = phase * tt + t
            base = t * nslot * ROW_PITCH
            acc = jnp.zeros((nslot, LANES), F32)
            for c in range(nchunk):
                uc = cur[pl.ds(base + c, nslot, stride=ROW_PITCH), :]
                acc = acc + uc * hm[tok:tok + 1, c * LANES:(c + 1) * LANES]
            a = jnp.sum(acc, axis=1, keepdims=True)
            w = jax.nn.gelu(a) * g_ref[0, :, tok:tok + 1]
            for c in range(nchunk):
                vc = cur[pl.ds(base + nchunk + c, nslot, stride=ROW_PITCH), :]
                y_ref[tok:tok + 1, c * LANES:(c + 1) * LANES] = jnp.sum(vc * w, axis=0, keepdims=True)

    r = alpha * x + g2_ref[0] * y_ref[...]
    o_ref[...] = _layer_norm(r) * gam_ref[...] + bet_ref[...]

    @pl.when(step == last)
    def _():
        tile_wait(buf_a, 0)


def peer_experts(alpha, x, shift, scale, g2, gamma, beta, idx_t, gates_t, uv):
    b, l, d = x.shape
    assert d == SUBLANES * LANES
    nslot = idx_t.shape[1]
    blk = 2 * PEER_TOK_TILE
    blocks_per_batch = l // blk
    nb = b * blocks_per_batch
    idx = idx_t.transpose(0, 2, 1).reshape(nb, blk, nslot)
    gates = gates_t.reshape(b, nslot, blocks_per_batch, blk).transpose(0, 2, 1, 3).reshape(nb, nslot, blk)
    per_batch = lambda i: (i // blocks_per_batch, 0, 0)
    vec = pl.BlockSpec((1, d), lambda i: (0, 0))
    gather_buf = pltpu.VMEM((PEER_TOK_TILE * nslot * ROW_PITCH, LANES), F32)
    out = pl.pallas_call(
        functools.partial(_peer_experts_kernel, alpha),
        grid=(nb,),
        in_specs=[pl.BlockSpec((1, blk, nslot), lambda i: (i, 0, 0), memory_space=pltpu.SMEM),
                  pl.BlockSpec((1, blk, nslot), lambda i: (jnp.minimum(i + 1, nb - 1), 0, 0),
                               memory_space=pltpu.SMEM),
                  pl.BlockSpec((1, nslot, blk), lambda i: (i, 0, 0)),
                  pl.BlockSpec((blk, d), lambda i: (i, 0)),
                  pl.BlockSpec((1, 1, d), per_batch),
                  pl.BlockSpec((1, 1, d), per_batch),
                  pl.BlockSpec((1, 1, d), per_batch),
                  vec, vec,
                  pl.BlockSpec(memory_space=pl.ANY)],
        out_specs=pl.BlockSpec((blk, d), lambda i: (i, 0)),
        out_shape=jax.ShapeDtypeStruct((b * l, d), F32),
        scratch_shapes=[gather_buf, gather_buf, pltpu.VMEM((blk, d), F32), pltpu.SemaphoreType.DMA((2,))],
        compiler_params=_cparams(("arbitrary",)),
    )(idx, idx, gates, x.reshape(b * l, d), shift, scale, g2, gamma, beta, uv)
    return out.reshape(b, l, d)


def _to_col_major(t, rows):
    b, l, c = t.shape
    return t.reshape(b, rows, GRID_W, c).transpose(0, 2, 1, 3).reshape(b, l, c)


def _from_col_major(t, rows):
    b, l, c = t.shape
    return t.reshape(b, GRID_W, rows, c).transpose(0, 2, 1, 3).reshape(b, l, c)


def _pad_heads(t):
    b, l, _ = t.shape
    t = t.reshape(b, l, GLA_HEADS, GLA_DK)
    t = jnp.pad(t, ((0, 0), (0, 0), (0, 0), (0, LANES - GLA_DK)))
    return t.reshape(b, l, GLA_HEADS * LANES)


def _gla_parts(u, rows):
    o = 5 * HG_HEADS * LANES
    kw = GLA_HEADS * GLA_DK
    vw = GLA_HEADS * GLA_DV
    order = (lambda t: t) if rows is None else (lambda t: _to_col_major(t, rows))
    q = order(_pad_heads(u[..., o:o + kw]))
    k = order(_pad_heads(u[..., o + kw:o + 2 * kw]))
    v = order(u[..., o + 2 * kw:o + 2 * kw + vw])
    r0 = o + 2 * kw + 2 * vw
    rf = order(u[..., r0:r0 + GLA_GATE_RANK])
    rb = order(u[..., r0 + GLA_GATE_RANK:r0 + 2 * GLA_GATE_RANK])
    return q, k, v, rf, rb


def _token_tile(l, want):
    while l % want:
        want //= 2
    return want


def kernel(x, c, ctx, c_ctx, w_ada, b_ada, w_in, w_gk2, b_gk, hg_lower_bounds, hg_norm, gla_norm, w_out,
           ln_gamma, ln_beta, peer_w_query, peer_sub_keys, peer_u, peer_v):
    batch, seq, d = x.shape
    depth = w_ada.shape[0]
    rows = seq // GRID_W
    alpha = (2.0 * depth) ** 0.25
    n_exp = peer_u.shape[1]

    sm = jax.nn.softmax(hg_lower_bounds.astype(F32), axis=1)
    lb_all = jnp.clip(jnp.cumsum(sm, axis=1) - sm[:, :1], 0.0, 1.0 - 1e-6)
    hg_consts = jnp.stack([jnp.log(jnp.maximum(lb_all, LB_FLOOR)), jnp.log1p(-lb_all), 1.0 - lb_all], axis=2)

    cs = jnp.concatenate([c, c_ctx[None, :], jnp.zeros((SUBLANES - 1 - batch % SUBLANES, d), F32)], axis=0)
    mods = ada_mods(cs, w_ada, b_ada)

    d_in = w_in.shape[2]
    d_in_pad = -(-d_in // LANES) * LANES
    zero_state = jnp.zeros((batch, HG_HEADS, LANES, LANES), F32)
    lat_tile = _token_tile(seq, 256)
    ctx_tile = _token_tile(ctx.shape[1], 256)
    gw_pad = lambda w: _pad_heads(w[None])[0]

    xc = ctx
    for i in range(depth):
        last = i == depth - 1
        m_l = mods[i, :batch].reshape(batch, 1, 6, d)
        m_c = jnp.broadcast_to(mods[i, batch].reshape(1, 1, 6, d), (batch, 1, 6, d))
        w_in_b = jnp.pad(w_in[i], ((0, 0), (0, d_in_pad - d_in))).astype(BF16)
        w_out_b = w_out[i].astype(BF16)
        w_q_b = peer_w_query[i].astype(BF16)
        gam, bet = ln_gamma[i], ln_beta[i]
        hgain, ggain = hg_norm[i][None, :], gla_norm[i][None, :]
        gk_w = [gw_pad(w_gk2[i, dd]) for dd in range(2)]
        gk_b = [gw_pad(b_gk[i, dd][None, :]) for dd in range(2)]
        uv = jnp.concatenate([peer_u[i].reshape(n_exp, SUBLANES, LANES),
                              peer_v[i].reshape(n_exp, SUBLANES, LANES)], axis=1)

        u_c = ln_mod_matmul(xc, m_c[:, :, 0], m_c[:, :, 1], w_in_b, ctx_tile)
        u_l = ln_mod_matmul(x, m_l[:, :, 0], m_l[:, :, 1], w_in_b, lat_tile)
        h_out, g_out = {}, {}
        gc = _gla_parts(u_c, None)
        gl = _gla_parts(u_l, rows)
        for dd, rev in ((0, False), (1, True)):
            o_c, s_c = hgrn_scan(u_c, 1 + dd, hg_consts[dd, i], zero_state, rev)
            o_l, _ = hgrn_scan(u_l, 1 + dd, hg_consts[dd, i], s_c, rev)
            h_out[dd] = (o_c, o_l)
            o_c, s_c = gla_scan(gc[0], gc[1], gc[2], gc[3 + dd], gk_w[dd], gk_b[dd], zero_state, rev)
            o_l, _ = gla_scan(gl[0], gl[1], gl[2], gl[3 + dd], gk_w[dd], gk_b[dd], s_c, rev)
            g_out[dd] = (o_c, _from_col_major(o_l, rows))
        x = mix_out(alpha, h_out[0][1], h_out[1][1], g_out[0][1], g_out[1][1], u_l, x, m_l[:, :, 2],
                    hgain, ggain, w_out_b, gam[0:1], bet[0:1], lat_tile)
        if not last:
            xc = mix_out(alpha, h_out[0][0], h_out[1][0], g_out[0][0], g_out[1][0], u_c, xc, m_c[:, :, 2],
                         hgain, ggain, w_out_b, gam[0:1], bet[0:1], ctx_tile)

        def peer_block(h, m, tile):
            qh = ln_mod_matmul(h, m[:, :, 3], m[:, :, 4], w_q_b, tile)
            idx_t, gates_t = peer_route(qh, peer_sub_keys[i], _token_tile(h.shape[1], LANES))
            return peer_experts(alpha, h, m[:, :, 3], m[:, :, 4], m[:, :, 5], gam[1:2], bet[1:2], idx_t, gates_t, uv)

        x = peer_block(x, m_l, lat_tile)
        if not last:
            xc = peer_block(xc, m_c, ctx_tile)
    return x
```

```python
import functools

import jax
import jax.numpy as jnp
from jax import lax
from jax.experimental import pallas as pl
from jax.experimental.pallas import tpu as pltpu

F32 = jnp.float32
BF16 = jnp.bfloat16

GRID_W = 64
HG_HEADS = 4
HG_DK = 128
GLA_HEADS = 4
GLA_DK = 64
GLA_DV = 128
GLA_GATE_RANK = 16
GLA_GATE_NORM = 16.0
PEER_HEADS = 8
PEER_NKEYS = 128
PEER_TOPK = 16
EPS = 1e-6
LB_FLOOR = 1e-30

LANES = 128
SUBLANES = 8
VMEM_LIMIT = 56 * 1024 * 1024

_NT = (((1,), (1,)), ((), ()))
_TN = (((0,), (0,)), ((), ()))


def _cparams(sem):
    return pltpu.CompilerParams(dimension_semantics=sem, vmem_limit_bytes=VMEM_LIMIT)


def _layer_norm(x):
    mu = jnp.mean(x, axis=-1, keepdims=True)
    xc = x - mu
    var = jnp.mean(xc * xc, axis=-1, keepdims=True)
    return xc * lax.rsqrt(var + EPS)


def _silu(x):
    return x * jax.nn.sigmoid(x)


def _ada_kernel(c_ref, w_ref, b_ref, o_ref):
    s = _silu(c_ref[...]).astype(BF16)
    o_ref[0] = jnp.dot(s, w_ref[0].astype(BF16), preferred_element_type=F32) + b_ref[0]


def ada_mods(cs, w_ada, b_ada):
    depth, d, n = w_ada.shape
    r = cs.shape[0]
    tn = n // 4
    return pl.pallas_call(
        _ada_kernel,
        grid=(depth, n // tn),
        in_specs=[pl.BlockSpec((r, d), lambda l, j: (0, 0)),
                  pl.BlockSpec((1, d, tn), lambda l, j: (l, 0, j)),
                  pl.BlockSpec((1, 1, tn), lambda l, j: (l, 0, j))],
        out_specs=pl.BlockSpec((1, r, tn), lambda l, j: (l, 0, j)),
        out_shape=jax.ShapeDtypeStruct((depth, r, n), F32),
        compiler_params=_cparams(("parallel", "parallel")),
    )(cs, w_ada, b_ada.reshape(depth, 1, n))


def _ln_mod_matmul_kernel(x_ref, sh_ref, sc_ref, w_ref, o_ref):
    y = _layer_norm(x_ref[0]) * (1.0 + sc_ref[0]) + sh_ref[0]
    o_ref[0] = jnp.dot(y.astype(BF16), w_ref[...], preferred_element_type=F32)


def ln_mod_matmul(x, shift, scale, w_bf16, tile):
    b, l, d = x.shape
    n = w_bf16.shape[1]
    return pl.pallas_call(
        _ln_mod_matmul_kernel,
        grid=(b, l // tile),
        in_specs=[pl.BlockSpec((1, tile, d), lambda i, j: (i, j, 0)),
                  pl.BlockSpec((1, 1, d), lambda i, j: (i, 0, 0)),
                  pl.BlockSpec((1, 1, d), lambda i, j: (i, 0, 0)),
                  pl.BlockSpec((d, n), lambda i, j: (0, 0))],
        out_specs=pl.BlockSpec((1, tile, n), lambda i, j: (i, j, 0)),
        out_shape=jax.ShapeDtypeStruct((b, l, n), F32),
        compiler_params=_cparams(("parallel", "parallel")),
    )(x, shift, scale, w_bf16)


def _chunk_scan(q, k, v, la, st_ref, reverse):
    c, dk = q.shape
    nlev = c.bit_length() - 1
    row = lax.broadcasted_iota(jnp.int32, (c, dk), 0)
    qi = lax.broadcasted_iota(jnp.int32, (c, c), 0)
    ki = lax.broadcasted_iota(jnp.int32, (c, c), 1)
    late, early = (ki, qi) if reverse else (qi, ki)
    scores = jnp.where(qi == ki,
                       lax.dot_general(q.astype(BF16), k.astype(BF16), _NT, preferred_element_type=F32), 0.0)
    p = la
    t = la
    for lev in range(nlev):
        m = 1 << lev
        right = (row & m) != 0
        if reverse:
            d = jnp.where(right, p - la, t - p + la)
        else:
            d = jnp.where(right, p, t - p)
        e = jnp.exp(d)
        s_l = lax.dot_general((q * e).astype(BF16), (k * e).astype(BF16), _NT, preferred_element_type=F32)
        lb = late >> lev
        mask = (lb == (early >> lev) + 1) & ((lb & 1) == 1)
        scores = jnp.where(mask, s_l, scores)
        sib = jnp.where(right, pltpu.roll(t, m, 0), pltpu.roll(t, c - m, 0))
        p = p + jnp.where(right, sib, 0.0)
        t = t + sib
    if reverse:
        dq, dkk = t - p + la, p - la
    else:
        dq, dkk = p, t - p
    st = st_ref[...]
    qs = (q * jnp.exp(dq)).astype(BF16)
    ks = (k * jnp.exp(dkk)).astype(BF16)
    vb = v.astype(BF16)
    o = (jnp.dot(scores.astype(BF16), vb, preferred_element_type=F32)
         + lax.dot_general(qs, st.astype(BF16), _NT, preferred_element_type=F32))
    st_ref[...] = st * jnp.exp(t[0:1, :]) + lax.dot_general(vb, ks, _TN, preferred_element_type=F32)
    return o


def _softplus_neg_abs(d):
    return jnp.log(1.0 + jnp.exp(-jnp.abs(d)))


def _log_sigmoid(z):
    return jnp.minimum(z, 0.0) - _softplus_neg_abs(z)


def _hgrn_gate(z, c0, c1, om):
    ls = _log_sigmoid(z)
    y = c1 + ls
    la = jnp.maximum(c0, y) + _softplus_neg_abs(c0 - y)
    return om * jax.nn.sigmoid(-z), la


def _scan_prologue(s0_ref, st_ref):
    @pl.when(pl.program_id(1) == 0)
    def _():
        st_ref[...] = s0_ref[0]


def _scan_epilogue(sfin_ref, st_ref):
    @pl.when(pl.program_id(1) == pl.num_programs(1) - 1)
    def _():
        sfin_ref[0] = st_ref[...]


def _head(ref, h):
    return ref[0, :, h * LANES:(h + 1) * LANES]


def _hgrn_scan_kernel(reverse, q_ref, v_ref, z_ref, cst_ref, s0_ref, o_ref, sfin_ref, st_ref):
    _scan_prologue(s0_ref, st_ref)
    for h in range(HG_HEADS):
        cols = slice(h * LANES, (h + 1) * LANES)
        k, la = _hgrn_gate(_head(z_ref, h), cst_ref[0:1, cols], cst_ref[1:2, cols], cst_ref[2:3, cols])
        o_ref[0, :, cols] = _chunk_scan(_head(q_ref, h), k, _head(v_ref, h), la, st_ref.at[h], reverse)
    _scan_epilogue(sfin_ref, st_ref)


def _gla_scan_kernel(reverse, q_ref, k_ref, v_ref, r_ref, w_ref, b_ref, s0_ref, o_ref, sfin_ref, st_ref):
    _scan_prologue(s0_ref, st_ref)
    g = jnp.dot(r_ref[0].astype(BF16), w_ref[...].astype(BF16), preferred_element_type=F32) + b_ref[...]
    la_all = _log_sigmoid(g) * (1.0 / GLA_GATE_NORM)
    for h in range(GLA_HEADS):
        cols = slice(h * LANES, (h + 1) * LANES)
        q = _head(q_ref, h) * (GLA_DK ** -0.5)
        o_ref[0, :, cols] = _chunk_scan(q, _head(k_ref, h), _head(v_ref, h), la_all[:, cols], st_ref.at[h], reverse)
    _scan_epilogue(sfin_ref, st_ref)


def _scan_chunk(l):
    return 128 if l % 128 == 0 else 64


def _scan_call(kernel, reverse, batch, l, heads, in_arrays, in_specs, s0):
    c = _scan_chunk(l)
    n = l // c
    width = heads * LANES
    state_spec = pl.BlockSpec((1, heads, LANES, LANES), lambda b, j: (b, 0, 0, 0))
    return pl.pallas_call(
        functools.partial(kernel, reverse),
        grid=(batch, n),
        in_specs=in_specs + [state_spec],
        out_specs=[pl.BlockSpec((1, c, width), lambda b, j: (b, (n - 1 - j) if reverse else j, 0)),
                   state_spec],
        out_shape=[jax.ShapeDtypeStruct((batch, l, width), F32),
                   jax.ShapeDtypeStruct((batch, heads, LANES, LANES), F32)],
        scratch_shapes=[pltpu.VMEM((heads, LANES, LANES), F32)],
        compiler_params=_cparams(("parallel", "arbitrary")),
    )(*in_arrays, s0)


def hgrn_scan(u, gate_block, consts, s0, reverse):
    b, l, _ = u.shape
    c = _scan_chunk(l)
    n = l // c
    width = HG_HEADS * LANES

    def col(block):
        return pl.BlockSpec((1, c, width), lambda i, j: (i, (n - 1 - j) if reverse else j, block))

    in_specs = [col(0), col(3), col(gate_block), pl.BlockSpec((3, width), lambda i, j: (0, 0))]
    return _scan_call(_hgrn_scan_kernel, reverse, b, l, HG_HEADS, (u, u, u, consts), in_specs, s0)


def gla_scan(q, k, v, r, w, bias, s0, reverse):
    b, l, width = q.shape
    c = _scan_chunk(l)
    n = l // c
    rank = r.shape[-1]

    def col(wd):
        return pl.BlockSpec((1, c, wd), lambda i, j: (i, (n - 1 - j) if reverse else j, 0))

    in_specs = [col(width), col(width), col(width), col(rank),
                pl.BlockSpec((rank, width), lambda i, j: (0, 0)),
                pl.BlockSpec((1, width), lambda i, j: (0, 0))]
    return _scan_call(_gla_scan_kernel, reverse, b, l, GLA_HEADS, (q, k, v, r, w, bias), in_specs, s0)


def _head_rms(a, gain, heads):
    outs = []
    for h in range(heads):
        ah = a[:, h * LANES:(h + 1) * LANES]
        outs.append(ah * lax.rsqrt(jnp.mean(ah * ah, axis=-1, keepdims=True) + EPS) * gain)
    return jnp.concatenate(outs, axis=1)


def _mix_out_kernel(alpha, hf_ref, hb_ref, gf_ref, gb_ref, hgate_ref, ggate_ref, x_ref, g1_ref,
                    hgain_ref, ggain_ref, w_ref, gam_ref, bet_ref, o_ref):
    ma = _head_rms(hf_ref[0] + hb_ref[0], hgain_ref[...], HG_HEADS) * _silu(hgate_ref[0])
    mb = _head_rms(gf_ref[0] + gb_ref[0], ggain_ref[...], GLA_HEADS) * _silu(ggate_ref[0])
    mix = jnp.concatenate([ma, mb], axis=1).astype(BF16)
    y = jnp.dot(mix, w_ref[...], preferred_element_type=F32)
    r = alpha * x_ref[0] + g1_ref[0] * y
    o_ref[0] = _layer_norm(r) * gam_ref[...] + bet_ref[...]


def mix_out(alpha, hf, hb, gf, gb, u, x, g1, hgain, ggain, w_out_bf16, gamma, beta, tile):
    b, l, d = x.shape
    hw = HG_HEADS * LANES
    tok = lambda width, blk: pl.BlockSpec((1, tile, width), lambda i, j: (i, j, blk))
    vec = lambda width: pl.BlockSpec((1, width), lambda i, j: (0, 0))
    return pl.pallas_call(
        functools.partial(_mix_out_kernel, alpha),
        grid=(b, l // tile),
        in_specs=[tok(hw, 0), tok(hw, 0), tok(hw, 0), tok(hw, 0),
                  tok(hw, 4), tok(hw, 7), tok(d, 0),
                  pl.BlockSpec((1, 1, d), lambda i, j: (i, 0, 0)),
                  vec(LANES), vec(LANES),
                  pl.BlockSpec(w_out_bf16.shape, lambda i, j: (0, 0)),
                  vec(d), vec(d)],
        out_specs=tok(d, 0),
        out_shape=jax.ShapeDtypeStruct((b, l, d), F32),
        compiler_params=_cparams(("parallel", "parallel")),
    )(hf, hb, gf, gb, u, u, x, g1, hgain, ggain, w_out_bf16, gamma, beta)


_CAND_PIECES = ((0, 16),) + tuple((a, 8) for a in range(1, 8))
_CAND_TAIL = 8
_NCAND = sum(n for _, n in _CAND_PIECES) + _CAND_TAIL


def _extract_top(s, iota, nrows):
    m = jnp.max(s, axis=0, keepdims=True)
    pos = jnp.min(jnp.where(s == m, iota, nrows), axis=0, keepdims=True)
    return m, pos, iota == pos


def _peer_route_kernel(q_ref, k1_ref, k2_ref, idx_ref, g_ref, sc, v1, i1, v2, i2, cs, ci, tv):
    tt = q_ref.shape[1]
    nk, topk = PEER_NKEYS, PEER_TOPK
    for h in range(PEER_HEADS):
        o = 2 * nk * h
        sc[2 * h] = lax.dot_general(k1_ref[...], q_ref[0, :, o:o + nk], _NT,
                                    precision=lax.Precision.HIGHEST, preferred_element_type=F32)
        sc[2 * h + 1] = lax.dot_general(k2_ref[...], q_ref[0, :, o + nk:o + 2 * nk], _NT,
                                        precision=lax.Precision.HIGHEST, preferred_element_type=F32)
    iota = lax.broadcasted_iota(jnp.int32, (nk, tt), 0)
    iota2 = lax.broadcasted_iota(jnp.int32, (_NCAND, tt), 0)

    def per_head(h, carry):
        def first_stage(j, st):
            s1, s2 = st
            m1, p1, hit1 = _extract_top(s1, iota, nk)
            m2, p2, hit2 = _extract_top(s2, iota, nk)
            v1[pl.ds(j, 1), :] = m1
            i1[pl.ds(j, 1), :] = p1
            v2[pl.ds(j, 1), :] = m2
            i2[pl.ds(j, 1), :] = p2
            return jnp.where(hit1, -jnp.inf, s1), jnp.where(hit2, -jnp.inf, s2)

        lax.fori_loop(0, topk, first_stage, (sc[2 * h], sc[2 * h + 1]))
        row = 0
        for a, n in _CAND_PIECES:
            cs[row:row + n, :] = v1[a:a + 1, :] + v2[0:n, :]
            ci[row:row + n, :] = i1[a:a + 1, :] * nk + i2[0:n, :]
            row += n
        cs[row:row + _CAND_TAIL, :] = v1[topk - _CAND_TAIL:topk, :] + v2[0:1, :]
        ci[row:row + _CAND_TAIL, :] = i1[topk - _CAND_TAIL:topk, :] * nk + i2[0:1, :]
        out0 = pl.multiple_of(h * topk, topk)

        def second_stage(j, s):
            m, _, hit = _extract_top(s, iota2, _NCAND)
            tv[pl.ds(j, 1), :] = m
            idx_ref[0, pl.ds(out0 + j, 1), :] = jnp.max(jnp.where(hit, ci[...], -1), axis=0, keepdims=True)
            return jnp.where(hit, -jnp.inf, s)

        lax.fori_loop(0, topk, second_stage, cs[...])
        top = tv[...]
        e = jnp.exp(top - top[0:1, :])
        g_ref[0, pl.ds(out0, topk), :] = e / jnp.sum(e, axis=0, keepdims=True)
        return carry

    lax.fori_loop(0, PEER_HEADS, per_head, 0)


def peer_route(q, sub_keys, tile):
    b, l, dq = q.shape
    nslot = PEER_HEADS * PEER_TOPK
    out_spec = pl.BlockSpec((1, nslot, tile), lambda i, j: (i, 0, j))
    keys = pl.BlockSpec((PEER_NKEYS, PEER_NKEYS), lambda i, j: (0, 0))
    return pl.pallas_call(
        _peer_route_kernel,
        grid=(b, l // tile),
        in_specs=[pl.BlockSpec((1, tile, dq), lambda i, j: (i, j, 0)), keys, keys],
        out_specs=[out_spec, out_spec],
        out_shape=[jax.ShapeDtypeStruct((b, nslot, l), jnp.int32),
                   jax.ShapeDtypeStruct((b, nslot, l), F32)],
        scratch_shapes=[pltpu.VMEM((2 * PEER_HEADS, PEER_NKEYS, tile), F32),
                        pltpu.VMEM((PEER_TOPK, tile), F32), pltpu.VMEM((PEER_TOPK, tile), jnp.int32),
                        pltpu.VMEM((PEER_TOPK, tile), F32), pltpu.VMEM((PEER_TOPK, tile), jnp.int32),
                        pltpu.VMEM((_NCAND, tile), F32), pltpu.VMEM((_NCAND, tile), jnp.int32),
                        pltpu.VMEM((PEER_TOPK, tile), F32)],
        compiler_params=_cparams(("parallel", "parallel")),
    )(q, sub_keys[0], sub_keys[1])


PEER_TOK_TILE = 8
PEER_ISSUE_SLACK = 2
ROW_WORDS = SUBLANES
PAIR_ROWS = ROW_WORDS // 2


def _peer_experts_kernel(alpha, ids_ref, ids_next_ref, g_ref, x_ref, sh_ref, sc_ref, g2_ref, gam_ref, bet_ref,
                         uv_hbm, o_ref, buf_a, buf_b, y_ref, sem):
    tt = PEER_TOK_TILE
    d = x_ref.shape[1]
    nslot = g_ref.shape[1]
    step = pl.program_id(0)
    last = pl.num_programs(0) - 1
    bufs = (buf_a, buf_b)

    def row_copy(expert, dst, row, s):
        return pltpu.make_async_copy(uv_hbm.at[expert], dst.at[pl.ds(row, ROW_WORDS)], sem.at[s])

    def tile_wait(buf, s):
        pltpu.make_async_copy(buf, buf, sem.at[s]).wait()

    def unpack(words):
        lo = pltpu.bitcast(words << 16, F32)
        hi = pltpu.bitcast(words & jnp.uint32(0xFFFF0000), F32)
        return lo, hi

    @pl.when(step == 0)
    def _():
        def per_token(t, carry):
            def per_group(g, carry):
                for j in range(SUBLANES):
                    k = g * SUBLANES + j
                    row = pl.multiple_of((t * nslot + k) * ROW_WORDS, ROW_WORDS)
                    row_copy(ids_ref[0, t, k], buf_a, row, 0).start(priority=j % 2)
                return carry
            return lax.fori_loop(0, nslot // SUBLANES, per_group, carry)
        lax.fori_loop(0, tt, per_token, 0)

    x = x_ref[...]
    hm = _layer_norm(x) * (1.0 + sc_ref[0]) + sh_ref[0]

    issue_tokens = tt - PEER_ISSUE_SLACK
    per_token_issue = -(-tt * nslot // issue_tokens)

    for phase in range(2):
        cur, nxt = bufs[phase], bufs[1 - phase]
        tile_wait(cur, phase)
        for t in range(tt):
            for n in range(t * per_token_issue, min((t + 1) * per_token_issue, tt * nslot)):
                tn, k = divmod(n, nslot)
                expert = ids_ref[0, tt + tn, k] if phase == 0 else ids_next_ref[0, tn, k]
                row_copy(expert, nxt, n * ROW_WORDS, 1 - phase).start(priority=n % 2)
            tok = phase * tt + t
            base = t * nslot * ROW_WORDS
            hrow = lambda c: hm[tok:tok + 1, c * LANES:(c + 1) * LANES]
            acc = jnp.zeros((nslot, LANES), F32)
            for j in range(PAIR_ROWS):
                lo, hi = unpack(cur[pl.ds(base + j, nslot, stride=ROW_WORDS), :])
                acc = acc + lo * hrow(2 * j) + hi * hrow(2 * j + 1)
            a = jnp.sum(acc, axis=1, keepdims=True)
            w = jax.nn.gelu(a) * g_ref[0, :, tok:tok + 1]
            for j in range(PAIR_ROWS):
                lo, hi = unpack(cur[pl.ds(base + PAIR_ROWS + j, nslot, stride=ROW_WORDS), :])
                for c, vc in ((2 * j, lo), (2 * j + 1, hi)):
                    y_ref[tok:tok + 1, c * LANES:(c + 1) * LANES] = jnp.sum(vc * w, axis=0, keepdims=True)

    r = alpha * x + g2_ref[0] * y_ref[...]
    o_ref[...] = _layer_norm(r) * gam_ref[...] + bet_ref[...]

    @pl.when(step == last)
    def _():
        tile_wait(buf_a, 0)


def pack_experts(u, v):
    def pack(a):
        bits = lax.bitcast_convert_type(a.astype(BF16).reshape(a.shape[0], PAIR_ROWS, 2, LANES), jnp.uint16)
        bits = bits.astype(jnp.uint32)
        return bits[:, :, 0, :] | (bits[:, :, 1, :] << 16)
    return jnp.concatenate([pack(u), pack(v)], axis=1)


def peer_experts(alpha, x, shift, scale, g2, gamma, beta, idx_t, gates_t, uv):
    b, l, d = x.shape
    assert d == 2 * PAIR_ROWS * LANES
    nslot = idx_t.shape[1]
    blk = 2 * PEER_TOK_TILE
    blocks_per_batch = l // blk
    nb = b * blocks_per_batch
    idx = idx_t.transpose(0, 2, 1).reshape(nb, blk, nslot)
    gates = gates_t.reshape(b, nslot, blocks_per_batch, blk).transpose(0, 2, 1, 3).reshape(nb, nslot, blk)
    per_batch = lambda i: (i // blocks_per_batch, 0, 0)
    vec = pl.BlockSpec((1, d), lambda i: (0, 0))
    gather_buf = pltpu.VMEM((PEER_TOK_TILE * nslot * ROW_WORDS, LANES), jnp.uint32)
    out = pl.pallas_call(
        functools.partial(_peer_experts_kernel, alpha),
        grid=(nb,),
        in_specs=[pl.BlockSpec((1, blk, nslot), lambda i: (i, 0, 0), memory_space=pltpu.SMEM),
                  pl.BlockSpec((1, blk, nslot), lambda i: (jnp.minimum(i + 1, nb - 1), 0, 0),
                               memory_space=pltpu.SMEM),
                  pl.BlockSpec((1, nslot, blk), lambda i: (i, 0, 0)),
                  pl.BlockSpec((blk, d), lambda i: (i, 0)),
                  pl.BlockSpec((1, 1, d), per_batch),
                  pl.BlockSpec((1, 1, d), per_batch),
                  pl.BlockSpec((1, 1, d), per_batch),
                  vec, vec,
                  pl.BlockSpec(memory_space=pl.ANY)],
        out_specs=pl.BlockSpec((blk, d), lambda i: (i, 0)),
        out_shape=jax.ShapeDtypeStruct((b * l, d), F32),
        scratch_shapes=[gather_buf, gather_buf, pltpu.VMEM((blk, d), F32), pltpu.SemaphoreType.DMA((2,))],
        compiler_params=_cparams(("arbitrary",)),
    )(idx, idx, gates, x.reshape(b * l, d), shift, scale, g2, gamma, beta, uv)
    return out.reshape(b, l, d)


def _to_col_major(t, rows):
    b, l, c = t.shape
    return t.reshape(b, rows, GRID_W, c).transpose(0, 2, 1, 3).reshape(b, l, c)


def _from_col_major(t, rows):
    b, l, c = t.shape
    return t.reshape(b, GRID_W, rows, c).transpose(0, 2, 1, 3).reshape(b, l, c)


def _pad_heads(t):
    b, l, _ = t.shape
    t = t.reshape(b, l, GLA_HEADS, GLA_DK)
    t = jnp.pad(t, ((0, 0), (0, 0), (0, 0), (0, LANES - GLA_DK)))
    return t.reshape(b, l, GLA_HEADS * LANES)


def _gla_parts(u, rows):
    o = 5 * HG_HEADS * LANES
    kw = GLA_HEADS * GLA_DK
    vw = GLA_HEADS * GLA_DV
    order = (lambda t: t) if rows is None else (lambda t: _to_col_major(t, rows))
    q = order(_pad_heads(u[..., o:o + kw]))
    k = order(_pad_heads(u[..., o + kw:o + 2 * kw]))
    v = order(u[..., o + 2 * kw:o + 2 * kw + vw])
    r0 = o + 2 * kw + 2 * vw
    rf = order(u[..., r0:r0 + GLA_GATE_RANK])
    rb = order(u[..., r0 + GLA_GATE_RANK:r0 + 2 * GLA_GATE_RANK])
    return q, k, v, rf, rb


def _token_tile(l, want):
    while l % want:
        want //= 2
    return want


def kernel(x, c, ctx, c_ctx, w_ada, b_ada, w_in, w_gk2, b_gk, hg_lower_bounds, hg_norm, gla_norm, w_out,
           ln_gamma, ln_beta, peer_w_query, peer_sub_keys, peer_u, peer_v):
    batch, seq, d = x.shape
    depth = w_ada.shape[0]
    rows = seq // GRID_W
    alpha = (2.0 * depth) ** 0.25

    sm = jax.nn.softmax(hg_lower_bounds.astype(F32), axis=1)
    lb_all = jnp.clip(jnp.cumsum(sm, axis=1) - sm[:, :1], 0.0, 1.0 - 1e-6)
    hg_consts = jnp.stack([jnp.log(jnp.maximum(lb_all, LB_FLOOR)), jnp.log1p(-lb_all), 1.0 - lb_all], axis=2)

    cs = jnp.concatenate([c, c_ctx[None, :], jnp.zeros((SUBLANES - 1 - batch % SUBLANES, d), F32)], axis=0)
    mods = ada_mods(cs, w_ada, b_ada)

    d_in = w_in.shape[2]
    d_in_pad = -(-d_in // LANES) * LANES
    zero_state = jnp.zeros((batch, HG_HEADS, LANES, LANES), F32)
    lat_tile = _token_tile(seq, 256)
    ctx_tile = _token_tile(ctx.shape[1], 256)
    gw_pad = lambda w: _pad_heads(w[None])[0]

    xc = ctx
    for i in range(depth):
        last = i == depth - 1
        m_l = mods[i, :batch].reshape(batch, 1, 6, d)
        m_c = jnp.broadcast_to(mods[i, batch].reshape(1, 1, 6, d), (batch, 1, 6, d))
        w_in_b = jnp.pad(w_in[i], ((0, 0), (0, d_in_pad - d_in))).astype(BF16)
        w_out_b = w_out[i].astype(BF16)
        w_q_b = peer_w_query[i].astype(BF16)
        gam, bet = ln_gamma[i], ln_beta[i]
        hgain, ggain = hg_norm[i][None, :], gla_norm[i][None, :]
        gk_w = [gw_pad(w_gk2[i, dd]) for dd in range(2)]
        gk_b = [gw_pad(b_gk[i, dd][None, :]) for dd in range(2)]
        uv = pack_experts(peer_u[i], peer_v[i])

        u_c = ln_mod_matmul(xc, m_c[:, :, 0], m_c[:, :, 1], w_in_b, ctx_tile)
        u_l = ln_mod_matmul(x, m_l[:, :, 0], m_l[:, :, 1], w_in_b, lat_tile)
        h_out, g_out = {}, {}
        gc = _gla_parts(u_c, None)
        gl = _gla_parts(u_l, rows)
        for dd, rev in ((0, False), (1, True)):
            o_c, s_c = hgrn_scan(u_c, 1 + dd, hg_consts[dd, i], zero_state, rev)
            o_l, _ = hgrn_scan(u_l, 1 + dd, hg_consts[dd, i], s_c, rev)
            h_out[dd] = (o_c, o_l)
            o_c, s_c = gla_scan(gc[0], gc[1], gc[2], gc[3 + dd], gk_w[dd], gk_b[dd], zero_state, rev)
            o_l, _ = gla_scan(gl[0], gl[1], gl[2], gl[3 + dd], gk_w[dd], gk_b[dd], s_c, rev)
            g_out[dd] = (o_c, _from_col_major(o_l, rows))
        x = mix_out(alpha, h_out[0][1], h_out[1][1], g_out[0][1], g_out[1][1], u_l, x, m_l[:, :, 2],
                    hgain, ggain, w_out_b, gam[0:1], bet[0:1], lat_tile)
        if not last:
            xc = mix_out(alpha, h_out[0][0], h_out[1][0], g_out[0][0], g_out[1][0], u_c, xc, m_c[:, :, 2],
                         hgain, ggain, w_out_b, gam[0:1], bet[0:1], ctx_tile)

        def peer_block(h, m, tile):
            qh = ln_mod_matmul(h, m[:, :, 3], m[:, :, 4], w_q_b, tile)
            idx_t, gates_t = peer_route(qh, peer_sub_keys[i], _token_tile(h.shape[1], LANES))
            return peer_experts(alpha, h, m[:, :, 3], m[:, :, 4], m[:, :, 5], gam[1:2], bet[1:2], idx_t, gates_t, uv)

        x = peer_block(x, m_l, lat_tile)
        if not last:
            xc = peer_block(xc, m_c, ctx_tile)
    return x
```

```python
import functools

import jax
import jax.numpy as jnp
from jax import lax
from jax.experimental import pallas as pl
from jax.experimental.pallas import tpu as pltpu
from jax.experimental.pallas import tpu_sc as plsc

F32 = jnp.float32
BF16 = jnp.bfloat16

GRID_W = 64
HG_HEADS = 4
HG_DK = 128
GLA_HEADS = 4
GLA_DK = 64
GLA_DV = 128
GLA_GATE_RANK = 16
GLA_GATE_NORM = 16.0
PEER_HEADS = 8
PEER_NKEYS = 128
PEER_TOPK = 16
EPS = 1e-6
LB_FLOOR = 1e-30

LANES = 128
SUBLANES = 8
VMEM_LIMIT = 56 * 1024 * 1024

_NT = (((1,), (1,)), ((), ()))
_TN = (((0,), (0,)), ((), ()))


def _cparams(sem):
    return pltpu.CompilerParams(dimension_semantics=sem, vmem_limit_bytes=VMEM_LIMIT)


def _layer_norm(x):
    mu = jnp.mean(x, axis=-1, keepdims=True)
    xc = x - mu
    var = jnp.mean(xc * xc, axis=-1, keepdims=True)
    return xc * lax.rsqrt(var + EPS)


def _silu(x):
    return x * jax.nn.sigmoid(x)


def _ada_kernel(c_ref, w_ref, b_ref, o_ref):
    s = _silu(c_ref[...]).astype(BF16)
    o_ref[0] = jnp.dot(s, w_ref[0].astype(BF16), preferred_element_type=F32) + b_ref[0]


def ada_mods(cs, w_ada, b_ada):
    depth, d, n = w_ada.shape
    r = cs.shape[0]
    tn = n // 4
    return pl.pallas_call(
        _ada_kernel,
        grid=(depth, n // tn),
        in_specs=[pl.BlockSpec((r, d), lambda l, j: (0, 0)),
                  pl.BlockSpec((1, d, tn), lambda l, j: (l, 0, j)),
                  pl.BlockSpec((1, 1, tn), lambda l, j: (l, 0, j))],
        out_specs=pl.BlockSpec((1, r, tn), lambda l, j: (l, 0, j)),
        out_shape=jax.ShapeDtypeStruct((depth, r, n), F32),
        compiler_params=_cparams(("parallel", "parallel")),
    )(cs, w_ada, b_ada.reshape(depth, 1, n))


def _ln_mod_matmul_kernel(x_ref, sh_ref, sc_ref, w_ref, o_ref):
    y = _layer_norm(x_ref[0]) * (1.0 + sc_ref[0]) + sh_ref[0]
    o_ref[0] = jnp.dot(y.astype(BF16), w_ref[...], preferred_element_type=F32)


def ln_mod_matmul(x, shift, scale, w_bf16, tile):
    b, l, d = x.shape
    n = w_bf16.shape[1]
    return pl.pallas_call(
        _ln_mod_matmul_kernel,
        grid=(b, l // tile),
        in_specs=[pl.BlockSpec((1, tile, d), lambda i, j: (i, j, 0)),
                  pl.BlockSpec((1, 1, d), lambda i, j: (i, 0, 0)),
                  pl.BlockSpec((1, 1, d), lambda i, j: (i, 0, 0)),
                  pl.BlockSpec((d, n), lambda i, j: (0, 0))],
        out_specs=pl.BlockSpec((1, tile, n), lambda i, j: (i, j, 0)),
        out_shape=jax.ShapeDtypeStruct((b, l, n), F32),
        compiler_params=_cparams(("parallel", "parallel")),
    )(x, shift, scale, w_bf16)


def _chunk_scan(q, k, v, la, st_ref, reverse):
    c, dk = q.shape
    nlev = c.bit_length() - 1
    row = lax.broadcasted_iota(jnp.int32, (c, dk), 0)
    qi = lax.broadcasted_iota(jnp.int32, (c, c), 0)
    ki = lax.broadcasted_iota(jnp.int32, (c, c), 1)
    late, early = (ki, qi) if reverse else (qi, ki)
    scores = jnp.where(qi == ki,
                       lax.dot_general(q.astype(BF16), k.astype(BF16), _NT, preferred_element_type=F32), 0.0)
    p = la
    t = la
    for lev in range(nlev):
        m = 1 << lev
        right = (row & m) != 0
        if reverse:
            d = jnp.where(right, p - la, t - p + la)
        else:
            d = jnp.where(right, p, t - p)
        e = jnp.exp(d)
        s_l = lax.dot_general((q * e).astype(BF16), (k * e).astype(BF16), _NT, preferred_element_type=F32)
        lb = late >> lev
        mask = (lb == (early >> lev) + 1) & ((lb & 1) == 1)
        scores = jnp.where(mask, s_l, scores)
        sib = jnp.where(right, pltpu.roll(t, m, 0), pltpu.roll(t, c - m, 0))
        p = p + jnp.where(right, sib, 0.0)
        t = t + sib
    if reverse:
        dq, dkk = t - p + la, p - la
    else:
        dq, dkk = p, t - p
    st = st_ref[...]
    qs = (q * jnp.exp(dq)).astype(BF16)
    ks = (k * jnp.exp(dkk)).astype(BF16)
    vb = v.astype(BF16)
    o = (jnp.dot(scores.astype(BF16), vb, preferred_element_type=F32)
         + lax.dot_general(qs, st.astype(BF16), _NT, preferred_element_type=F32))
    st_ref[...] = st * jnp.exp(t[0:1, :]) + lax.dot_general(vb, ks, _TN, preferred_element_type=F32)
    return o


def _softplus_neg_abs(d):
    return jnp.log(1.0 + jnp.exp(-jnp.abs(d)))


def _log_sigmoid(z):
    return jnp.minimum(z, 0.0) - _softplus_neg_abs(z)


def _hgrn_gate(z, c0, c1, om):
    ls = _log_sigmoid(z)
    y = c1 + ls
    la = jnp.maximum(c0, y) + _softplus_neg_abs(c0 - y)
    return om * jax.nn.sigmoid(-z), la


def _scan_prologue(s0_ref, st_ref):
    @pl.when(pl.program_id(1) == 0)
    def _():
        st_ref[...] = s0_ref[0]


def _scan_epilogue(sfin_ref, st_ref):
    @pl.when(pl.program_id(1) == pl.num_programs(1) - 1)
    def _():
        sfin_ref[0] = st_ref[...]


def _head(ref, h):
    return ref[0, :, h * LANES:(h + 1) * LANES]


def _hgrn_scan_kernel(reverse, q_ref, v_ref, z_ref, cst_ref, s0_ref, o_ref, sfin_ref, st_ref):
    _scan_prologue(s0_ref, st_ref)
    for h in range(HG_HEADS):
        cols = slice(h * LANES, (h + 1) * LANES)
        k, la = _hgrn_gate(_head(z_ref, h), cst_ref[0:1, cols], cst_ref[1:2, cols], cst_ref[2:3, cols])
        o_ref[0, :, cols] = _chunk_scan(_head(q_ref, h), k, _head(v_ref, h), la, st_ref.at[h], reverse)
    _scan_epilogue(sfin_ref, st_ref)


def _gla_scan_kernel(reverse, q_ref, k_ref, v_ref, r_ref, w_ref, b_ref, s0_ref, o_ref, sfin_ref, st_ref):
    _scan_prologue(s0_ref, st_ref)
    g = jnp.dot(r_ref[0].astype(BF16), w_ref[...].astype(BF16), preferred_element_type=F32) + b_ref[...]
    la_all = _log_sigmoid(g) * (1.0 / GLA_GATE_NORM)
    for h in range(GLA_HEADS):
        cols = slice(h * LANES, (h + 1) * LANES)
        q = _head(q_ref, h) * (GLA_DK ** -0.5)
        o_ref[0, :, cols] = _chunk_scan(q, _head(k_ref, h), _head(v_ref, h), la_all[:, cols], st_ref.at[h], reverse)
    _scan_epilogue(sfin_ref, st_ref)


def _scan_chunk(l):
    return 128 if l % 128 == 0 else 64


def _scan_call(kernel, reverse, batch, l, heads, in_arrays, in_specs, s0):
    c = _scan_chunk(l)
    n = l // c
    width = heads * LANES
    state_spec = pl.BlockSpec((1, heads, LANES, LANES), lambda b, j: (b, 0, 0, 0))
    return pl.pallas_call(
        functools.partial(kernel, reverse),
        grid=(batch, n),
        in_specs=in_specs + [state_spec],
        out_specs=[pl.BlockSpec((1, c, width), lambda b, j: (b, (n - 1 - j) if reverse else j, 0)),
                   state_spec],
        out_shape=[jax.ShapeDtypeStruct((batch, l, width), F32),
                   jax.ShapeDtypeStruct((batch, heads, LANES, LANES), F32)],
        scratch_shapes=[pltpu.VMEM((heads, LANES, LANES), F32)],
        compiler_params=_cparams(("parallel", "arbitrary")),
    )(*in_arrays, s0)


def hgrn_scan(u, gate_block, consts, s0, reverse):
    b, l, _ = u.shape
    c = _scan_chunk(l)
    n = l // c
    width = HG_HEADS * LANES

    def col(block):
        return pl.BlockSpec((1, c, width), lambda i, j: (i, (n - 1 - j) if reverse else j, block))

    in_specs = [col(0), col(3), col(gate_block), pl.BlockSpec((3, width), lambda i, j: (0, 0))]
    return _scan_call(_hgrn_scan_kernel, reverse, b, l, HG_HEADS, (u, u, u, consts), in_specs, s0)


def gla_scan(q, k, v, r, w, bias, s0, reverse):
    b, l, width = q.shape
    c = _scan_chunk(l)
    n = l // c
    rank = r.shape[-1]

    def col(wd):
        return pl.BlockSpec((1, c, wd), lambda i, j: (i, (n - 1 - j) if reverse else j, 0))

    in_specs = [col(width), col(width), col(width), col(rank),
                pl.BlockSpec((rank, width), lambda i, j: (0, 0)),
                pl.BlockSpec((1, width), lambda i, j: (0, 0))]
    return _scan_call(_gla_scan_kernel, reverse, b, l, GLA_HEADS, (q, k, v, r, w, bias), in_specs, s0)


def _head_rms(a, gain, heads):
    outs = []
    for h in range(heads):
        ah = a[:, h * LANES:(h + 1) * LANES]
        outs.append(ah * lax.rsqrt(jnp.mean(ah * ah, axis=-1, keepdims=True) + EPS) * gain)
    return jnp.concatenate(outs, axis=1)


def _mix_out_kernel(alpha, hf_ref, hb_ref, gf_ref, gb_ref, hgate_ref, ggate_ref, x_ref, g1_ref,
                    hgain_ref, ggain_ref, w_ref, gam_ref, bet_ref, o_ref):
    ma = _head_rms(hf_ref[0] + hb_ref[0], hgain_ref[...], HG_HEADS) * _silu(hgate_ref[0])
    mb = _head_rms(gf_ref[0] + gb_ref[0], ggain_ref[...], GLA_HEADS) * _silu(ggate_ref[0])
    mix = jnp.concatenate([ma, mb], axis=1).astype(BF16)
    y = jnp.dot(mix, w_ref[...], preferred_element_type=F32)
    r = alpha * x_ref[0] + g1_ref[0] * y
    o_ref[0] = _layer_norm(r) * gam_ref[...] + bet_ref[...]


def mix_out(alpha, hf, hb, gf, gb, u, x, g1, hgain, ggain, w_out_bf16, gamma, beta, tile):
    b, l, d = x.shape
    hw = HG_HEADS * LANES
    tok = lambda width, blk: pl.BlockSpec((1, tile, width), lambda i, j: (i, j, blk))
    vec = lambda width: pl.BlockSpec((1, width), lambda i, j: (0, 0))
    return pl.pallas_call(
        functools.partial(_mix_out_kernel, alpha),
        grid=(b, l // tile),
        in_specs=[tok(hw, 0), tok(hw, 0), tok(hw, 0), tok(hw, 0),
                  tok(hw, 4), tok(hw, 7), tok(d, 0),
                  pl.BlockSpec((1, 1, d), lambda i, j: (i, 0, 0)),
                  vec(LANES), vec(LANES),
                  pl.BlockSpec(w_out_bf16.shape, lambda i, j: (0, 0)),
                  vec(d), vec(d)],
        out_specs=tok(d, 0),
        out_shape=jax.ShapeDtypeStruct((b, l, d), F32),
        compiler_params=_cparams(("parallel", "parallel")),
    )(hf, hb, gf, gb, u, u, x, g1, hgain, ggain, w_out_bf16, gamma, beta)


_CAND_PIECES = ((0, 16),) + tuple((a, 8) for a in range(1, 8))
_CAND_TAIL = 8
_NCAND = sum(n for _, n in _CAND_PIECES) + _CAND_TAIL


def _extract_top(s, iota, nrows):
    m = jnp.max(s, axis=0, keepdims=True)
    pos = jnp.min(jnp.where(s == m, iota, nrows), axis=0, keepdims=True)
    return m, pos, iota == pos


def _peer_route_kernel(q_ref, k1_ref, k2_ref, idx_ref, g_ref, sc, v1, i1, v2, i2, cs, ci, tv):
    tt = q_ref.shape[1]
    nk, topk = PEER_NKEYS, PEER_TOPK
    for h in range(PEER_HEADS):
        o = 2 * nk * h
        sc[2 * h] = lax.dot_general(k1_ref[...], q_ref[0, :, o:o + nk], _NT,
                                    precision=lax.Precision.HIGHEST, preferred_element_type=F32)
        sc[2 * h + 1] = lax.dot_general(k2_ref[...], q_ref[0, :, o + nk:o + 2 * nk], _NT,
                                        precision=lax.Precision.HIGHEST, preferred_element_type=F32)
    iota = lax.broadcasted_iota(jnp.int32, (nk, tt), 0)
    iota2 = lax.broadcasted_iota(jnp.int32, (_NCAND, tt), 0)

    def per_head(h, carry):
        def first_stage(j, st):
            s1, s2 = st
            m1, p1, hit1 = _extract_top(s1, iota, nk)
            m2, p2, hit2 = _extract_top(s2, iota, nk)
            v1[pl.ds(j, 1), :] = m1
            i1[pl.ds(j, 1), :] = p1
            v2[pl.ds(j, 1), :] = m2
            i2[pl.ds(j, 1), :] = p2
            return jnp.where(hit1, -jnp.inf, s1), jnp.where(hit2, -jnp.inf, s2)

        lax.fori_loop(0, topk, first_stage, (sc[2 * h], sc[2 * h + 1]))
        row = 0
        for a, n in _CAND_PIECES:
            cs[row:row + n, :] = v1[a:a + 1, :] + v2[0:n, :]
            ci[row:row + n, :] = i1[a:a + 1, :] * nk + i2[0:n, :]
            row += n
        cs[row:row + _CAND_TAIL, :] = v1[topk - _CAND_TAIL:topk, :] + v2[0:1, :]
        ci[row:row + _CAND_TAIL, :] = i1[topk - _CAND_TAIL:topk, :] * nk + i2[0:1, :]
        out0 = pl.multiple_of(h * topk, topk)

        def second_stage(j, s):
            m, _, hit = _extract_top(s, iota2, _NCAND)
            tv[pl.ds(j, 1), :] = m
            idx_ref[0, pl.ds(out0 + j, 1), :] = jnp.max(jnp.where(hit, ci[...], -1), axis=0, keepdims=True)
            return jnp.where(hit, -jnp.inf, s)

        lax.fori_loop(0, topk, second_stage, cs[...])
        top = tv[...]
        e = jnp.exp(top - top[0:1, :])
        g_ref[0, pl.ds(out0, topk), :] = e / jnp.sum(e, axis=0, keepdims=True)
        return carry

    lax.fori_loop(0, PEER_HEADS, per_head, 0)


def peer_route(q, sub_keys, tile):
    b, l, dq = q.shape
    nslot = PEER_HEADS * PEER_TOPK
    out_spec = pl.BlockSpec((1, nslot, tile), lambda i, j: (i, 0, j))
    keys = pl.BlockSpec((PEER_NKEYS, PEER_NKEYS), lambda i, j: (0, 0))
    return pl.pallas_call(
        _peer_route_kernel,
        grid=(b, l // tile),
        in_specs=[pl.BlockSpec((1, tile, dq), lambda i, j: (i, j, 0)), keys, keys],
        out_specs=[out_spec, out_spec],
        out_shape=[jax.ShapeDtypeStruct((b, nslot, l), jnp.int32),
                   jax.ShapeDtypeStruct((b, nslot, l), F32)],
        scratch_shapes=[pltpu.VMEM((2 * PEER_HEADS, PEER_NKEYS, tile), F32),
                        pltpu.VMEM((PEER_TOPK, tile), F32), pltpu.VMEM((PEER_TOPK, tile), jnp.int32),
                        pltpu.VMEM((PEER_TOPK, tile), F32), pltpu.VMEM((PEER_TOPK, tile), jnp.int32),
                        pltpu.VMEM((_NCAND, tile), F32), pltpu.VMEM((_NCAND, tile), jnp.int32),
                        pltpu.VMEM((PEER_TOPK, tile), F32)],
        compiler_params=_cparams(("parallel", "parallel")),
    )(q, sub_keys[0], sub_keys[1])


PEER_TOK_TILE = 8
PEER_ISSUE_SLACK = 2
ROW_WORDS = SUBLANES
PAIR_ROWS = ROW_WORDS // 2


def _unpack_pairs(words):
    lo = pltpu.bitcast(words << 16, F32)
    hi = pltpu.bitcast(words & jnp.int32(-65536), F32)
    return lo, hi


def _expert_token(words, hm, g_ref, y_ref, tok):
    hrow = lambda c: hm[tok:tok + 1, c * LANES:(c + 1) * LANES]
    acc = None
    for j in range(PAIR_ROWS):
        lo, hi = _unpack_pairs(words(j))
        part = lo * hrow(2 * j) + hi * hrow(2 * j + 1)
        acc = part if acc is None else acc + part
    a = jnp.sum(acc, axis=1, keepdims=True)
    w = jax.nn.gelu(a) * g_ref[0, :, tok:tok + 1]
    for j in range(PAIR_ROWS):
        lo, hi = _unpack_pairs(words(PAIR_ROWS + j))
        for c, vc in ((2 * j, lo), (2 * j + 1, hi)):
            y_ref[tok:tok + 1, c * LANES:(c + 1) * LANES] = jnp.sum(vc * w, axis=0, keepdims=True)


def _peer_experts_kernel(alpha, ids_ref, ids_next_ref, g_ref, x_ref, sh_ref, sc_ref, g2_ref, gam_ref, bet_ref,
                         uv_hbm, o_ref, buf_a, buf_b, y_ref, sem):
    tt = PEER_TOK_TILE
    d = x_ref.shape[1]
    nslot = g_ref.shape[1]
    step = pl.program_id(0)
    last = pl.num_programs(0) - 1
    bufs = (buf_a, buf_b)

    def row_copy(expert, dst, row, s):
        return pltpu.make_async_copy(uv_hbm.at[expert], dst.at[pl.ds(row, ROW_WORDS)], sem.at[s])

    def tile_wait(buf, s):
        pltpu.make_async_copy(buf, buf, sem.at[s]).wait()

    @pl.when(step == 0)
    def _():
        def per_token(t, carry):
            def per_group(g, carry):
                for j in range(SUBLANES):
                    k = g * SUBLANES + j
                    row = pl.multiple_of((t * nslot + k) * ROW_WORDS, ROW_WORDS)
                    row_copy(ids_ref[0, t, k], buf_a, row, 0).start(priority=j % 2)
                return carry
            return lax.fori_loop(0, nslot // SUBLANES, per_group, carry)
        lax.fori_loop(0, tt, per_token, 0)

    x = x_ref[...]
    hm = _layer_norm(x) * (1.0 + sc_ref[0]) + sh_ref[0]

    issue_tokens = tt - PEER_ISSUE_SLACK
    per_token_issue = -(-tt * nslot // issue_tokens)

    for phase in range(2):
        cur, nxt = bufs[phase], bufs[1 - phase]
        tile_wait(cur, phase)
        for t in range(tt):
            for n in range(t * per_token_issue, min((t + 1) * per_token_issue, tt * nslot)):
                tn, k = divmod(n, nslot)
                expert = ids_ref[0, tt + tn, k] if phase == 0 else ids_next_ref[0, tn, k]
                row_copy(expert, nxt, n * ROW_WORDS, 1 - phase).start(priority=n % 2)
            tok = phase * tt + t
            base = t * nslot * ROW_WORDS
            words = lambda j, base=base, cur=cur: cur[pl.ds(base + j, nslot, stride=ROW_WORDS), :]
            _expert_token(words, hm, g_ref, y_ref, tok)

    r = alpha * x + g2_ref[0] * y_ref[...]
    o_ref[...] = _layer_norm(r) * gam_ref[...] + bet_ref[...]

    @pl.when(step == last)
    def _():
        tile_wait(buf_a, 0)


def pack_experts(u, v):
    def pack(a):
        bits = lax.bitcast_convert_type(a.astype(BF16).reshape(a.shape[0], PAIR_ROWS, 2, LANES), jnp.uint16)
        bits = bits.astype(jnp.uint32)
        return bits[:, :, 0, :] | (bits[:, :, 1, :] << 16)
    return lax.bitcast_convert_type(jnp.concatenate([pack(u), pack(v)], axis=1), jnp.int32)


def peer_experts(alpha, x, shift, scale, g2, gamma, beta, idx_t, gates_t, uv):
    b, l, d = x.shape
    assert d == 2 * PAIR_ROWS * LANES
    nslot = idx_t.shape[1]
    blk = 2 * PEER_TOK_TILE
    blocks_per_batch = l // blk
    nb = b * blocks_per_batch
    idx = idx_t.transpose(0, 2, 1).reshape(nb, blk, nslot)
    gates = gates_t.reshape(b, nslot, blocks_per_batch, blk).transpose(0, 2, 1, 3).reshape(nb, nslot, blk)
    per_batch = lambda i: (i // blocks_per_batch, 0, 0)
    vec = pl.BlockSpec((1, d), lambda i: (0, 0))
    gather_buf = pltpu.VMEM((PEER_TOK_TILE * nslot * ROW_WORDS, LANES), jnp.int32)
    out = pl.pallas_call(
        functools.partial(_peer_experts_kernel, alpha),
        grid=(nb,),
        in_specs=[pl.BlockSpec((1, blk, nslot), lambda i: (i, 0, 0), memory_space=pltpu.SMEM),
                  pl.BlockSpec((1, blk, nslot), lambda i: (jnp.minimum(i + 1, nb - 1), 0, 0),
                               memory_space=pltpu.SMEM),
                  pl.BlockSpec((1, nslot, blk), lambda i: (i, 0, 0)),
                  pl.BlockSpec((blk, d), lambda i: (i, 0)),
                  pl.BlockSpec((1, 1, d), per_batch),
                  pl.BlockSpec((1, 1, d), per_batch),
                  pl.BlockSpec((1, 1, d), per_batch),
                  vec, vec,
                  pl.BlockSpec(memory_space=pl.ANY)],
        out_specs=pl.BlockSpec((blk, d), lambda i: (i, 0)),
        out_shape=jax.ShapeDtypeStruct((b * l, d), F32),
        scratch_shapes=[gather_buf, gather_buf, pltpu.VMEM((blk, d), F32), pltpu.SemaphoreType.DMA((2,))],
        compiler_params=_cparams(("arbitrary",)),
    )(idx, idx, gates, x.reshape(b * l, d), shift, scale, g2, gamma, beta, uv)
    return out.reshape(b, l, d)


SC_CORES = 2
SC_SUBCORES = 16
SC_GATHER_ROWS = 64
PEER_SC_SHARE = 3 / 8
PEER_SC_MIN_TOKENS = 2048


def sc_gather_rows(table, idx):
    n = idx.shape[0]
    width = table.shape[1]
    workers = SC_CORES * SC_SUBCORES
    chunks = n // SC_GATHER_ROWS
    per_worker = chunks // workers
    assert n == per_worker * workers * SC_GATHER_ROWS

    def body(table_hbm, idx_hbm, out_hbm, idx_v, rows_v, sem):
        wid = lax.axis_index("s") * SC_CORES + lax.axis_index("c")

        @pl.loop(0, per_worker)
        def _(j):
            g = wid * per_worker + j
            pltpu.sync_copy(idx_hbm.at[g], idx_v)
            pltpu.async_copy(table_hbm.at[idx_v], rows_v, sem).wait()
            pltpu.sync_copy(rows_v, out_hbm.at[pl.ds(g * SC_GATHER_ROWS, SC_GATHER_ROWS)])

    return pl.kernel(
        body,
        out_type=jax.ShapeDtypeStruct((n, width), table.dtype),
        mesh=plsc.VectorSubcoreMesh(core_axis_name="c", subcore_axis_name="s"),
        scratch_types=[pltpu.VMEM((SC_GATHER_ROWS,), jnp.int32),
                       pltpu.VMEM((SC_GATHER_ROWS, width), table.dtype),
                       pltpu.SemaphoreType.DMA],
    )(table, idx.reshape(chunks, SC_GATHER_ROWS))


def _peer_dense_kernel(alpha, g_ref, x_ref, sh_ref, sc_ref, g2_ref, gam_ref, bet_ref, rows_ref, o_ref, y_ref):
    tt = x_ref.shape[0]
    nslot = g_ref.shape[1]
    x = x_ref[...]
    hm = _layer_norm(x) * (1.0 + sc_ref[0]) + sh_ref[0]
    for t in range(tt):
        words = lambda j, t=t: rows_ref[t * nslot:(t + 1) * nslot, j * LANES:(j + 1) * LANES]
        _expert_token(words, hm, g_ref, y_ref, t)
    r = alpha * x + g2_ref[0] * y_ref[...]
    o_ref[...] = _layer_norm(r) * gam_ref[...] + bet_ref[...]


def peer_dense(alpha, x, shift, scale, g2, gamma, beta, gates_t, rows):
    b, l, d = x.shape
    nslot = gates_t.shape[1]
    tt = PEER_TOK_TILE
    tiles_per_batch = l // tt
    nt = b * tiles_per_batch
    gates = gates_t.reshape(b, nslot, tiles_per_batch, tt).transpose(0, 2, 1, 3).reshape(nt, nslot, tt)
    per_batch = lambda i: (i // tiles_per_batch, 0, 0)
    vec = pl.BlockSpec((1, d), lambda i: (0, 0))
    out = pl.pallas_call(
        functools.partial(_peer_dense_kernel, alpha),
        grid=(nt,),
        in_specs=[pl.BlockSpec((1, nslot, tt), lambda i: (i, 0, 0)),
                  pl.BlockSpec((tt, d), lambda i: (i, 0)),
                  pl.BlockSpec((1, 1, d), per_batch),
                  pl.BlockSpec((1, 1, d), per_batch),
                  pl.BlockSpec((1, 1, d), per_batch),
                  vec, vec,
                  pl.BlockSpec((tt * nslot, ROW_WORDS * LANES), lambda i: (i, 0))],
        out_specs=pl.BlockSpec((tt, d), lambda i: (i, 0)),
        out_shape=jax.ShapeDtypeStruct((b * l, d), F32),
        scratch_shapes=[pltpu.VMEM((tt, d), F32)],
        compiler_params=_cparams(("parallel",)),
    )(gates, x.reshape(b * l, d), shift, scale, g2, gamma, beta, rows)
    return out.reshape(b, l, d)


def peer_experts_split(alpha, x, shift, scale, g2, gamma, beta, idx_t, gates_t, uv):
    l = x.shape[1]
    l_sc = int(l * PEER_SC_SHARE) // LANES * LANES if l >= PEER_SC_MIN_TOKENS else 0
    if l_sc == 0:
        return peer_experts(alpha, x, shift, scale, g2, gamma, beta, idx_t, gates_t, uv)
    l_tc = l - l_sc
    rows = sc_gather_rows(uv.reshape(uv.shape[0], ROW_WORDS * LANES),
                          idx_t[:, :, l_tc:].transpose(0, 2, 1).reshape(-1))
    y_tc = peer_experts(alpha, x[:, :l_tc], shift, scale, g2, gamma, beta,
                        idx_t[:, :, :l_tc], gates_t[:, :, :l_tc], uv)
    y_sc = peer_dense(alpha, x[:, l_tc:], shift, scale, g2, gamma, beta, gates_t[:, :, l_tc:], rows)
    return jnp.concatenate([y_tc, y_sc], axis=1)


def _to_col_major(t, rows):
    b, l, c = t.shape
    return t.reshape(b, rows, GRID_W, c).transpose(0, 2, 1, 3).reshape(b, l, c)


def _from_col_major(t, rows):
    b, l, c = t.shape
    return t.reshape(b, GRID_W, rows, c).transpose(0, 2, 1, 3).reshape(b, l, c)


def _pad_heads(t):
    b, l, _ = t.shape
    t = t.reshape(b, l, GLA_HEADS, GLA_DK)
    t = jnp.pad(t, ((0, 0), (0, 0), (0, 0), (0, LANES - GLA_DK)))
    return t.reshape(b, l, GLA_HEADS * LANES)


def _gla_parts(u, rows):
    o = 5 * HG_HEADS * LANES
    kw = GLA_HEADS * GLA_DK
    vw = GLA_HEADS * GLA_DV
    order = (lambda t: t) if rows is None else (lambda t: _to_col_major(t, rows))
    q = order(_pad_heads(u[..., o:o + kw]))
    k = order(_pad_heads(u[..., o + kw:o + 2 * kw]))
    v = order(u[..., o + 2 * kw:o + 2 * kw + vw])
    r0 = o + 2 * kw + 2 * vw
    rf = order(u[..., r0:r0 + GLA_GATE_RANK])
    rb = order(u[..., r0 + GLA_GATE_RANK:r0 + 2 * GLA_GATE_RANK])
    return q, k, v, rf, rb


def _token_tile(l, want):
    while l % want:
        want //= 2
    return want


def kernel(x, c, ctx, c_ctx, w_ada, b_ada, w_in, w_gk2, b_gk, hg_lower_bounds, hg_norm, gla_norm, w_out,
           ln_gamma, ln_beta, peer_w_query, peer_sub_keys, peer_u, peer_v):
    batch, seq, d = x.shape
    depth = w_ada.shape[0]
    rows = seq // GRID_W
    alpha = (2.0 * depth) ** 0.25

    sm = jax.nn.softmax(hg_lower_bounds.astype(F32), axis=1)
    lb_all = jnp.clip(jnp.cumsum(sm, axis=1) - sm[:, :1], 0.0, 1.0 - 1e-6)
    hg_consts = jnp.stack([jnp.log(jnp.maximum(lb_all, LB_FLOOR)), jnp.log1p(-lb_all), 1.0 - lb_all], axis=2)

    cs = jnp.concatenate([c, c_ctx[None, :], jnp.zeros((SUBLANES - 1 - batch % SUBLANES, d), F32)], axis=0)
    mods = ada_mods(cs, w_ada, b_ada)

    d_in = w_in.shape[2]
    d_in_pad = -(-d_in // LANES) * LANES
    zero_state = jnp.zeros((batch, HG_HEADS, LANES, LANES), F32)
    lat_tile = _token_tile(seq, 256)
    ctx_tile = _token_tile(ctx.shape[1], 256)
    gw_pad = lambda w: _pad_heads(w[None])[0]

    xc = ctx
    for i in range(depth):
        last = i == depth - 1
        m_l = mods[i, :batch].reshape(batch, 1, 6, d)
        m_c = jnp.broadcast_to(mods[i, batch].reshape(1, 1, 6, d), (batch, 1, 6, d))
        w_in_b = jnp.pad(w_in[i], ((0, 0), (0, d_in_pad - d_in))).astype(BF16)
        w_out_b = w_out[i].astype(BF16)
        w_q_b = peer_w_query[i].astype(BF16)
        gam, bet = ln_gamma[i], ln_beta[i]
        hgain, ggain = hg_norm[i][None, :], gla_norm[i][None, :]
        gk_w = [gw_pad(w_gk2[i, dd]) for dd in range(2)]
        gk_b = [gw_pad(b_gk[i, dd][None, :]) for dd in range(2)]
        uv = pack_experts(peer_u[i], peer_v[i])

        u_c = ln_mod_matmul(xc, m_c[:, :, 0], m_c[:, :, 1], w_in_b, ctx_tile)
        u_l = ln_mod_matmul(x, m_l[:, :, 0], m_l[:, :, 1], w_in_b, lat_tile)
        h_out, g_out = {}, {}
        gc = _gla_parts(u_c, None)
        gl = _gla_parts(u_l, rows)
        for dd, rev in ((0, False), (1, True)):
            o_c, s_c = hgrn_scan(u_c, 1 + dd, hg_consts[dd, i], zero_state, rev)
            o_l, _ = hgrn_scan(u_l, 1 + dd, hg_consts[dd, i], s_c, rev)
            h_out[dd] = (o_c, o_l)
            o_c, s_c = gla_scan(gc[0], gc[1], gc[2], gc[3 + dd], gk_w[dd], gk_b[dd], zero_state, rev)
            o_l, _ = gla_scan(gl[0], gl[1], gl[2], gl[3 + dd], gk_w[dd], gk_b[dd], s_c, rev)
            g_out[dd] = (o_c, _from_col_major(o_l, rows))
        x = mix_out(alpha, h_out[0][1], h_out[1][1], g_out[0][1], g_out[1][1], u_l, x, m_l[:, :, 2],
                    hgain, ggain, w_out_b, gam[0:1], bet[0:1], lat_tile)
        if not last:
            xc = mix_out(alpha, h_out[0][0], h_out[1][0], g_out[0][0], g_out[1][0], u_c, xc, m_c[:, :, 2],
                         hgain, ggain, w_out_b, gam[0:1], bet[0:1], ctx_tile)

        def peer_block(h, m, tile):
            qh = ln_mod_matmul(h, m[:, :, 3], m[:, :, 4], w_q_b, tile)
            idx_t, gates_t = peer_route(qh, peer_sub_keys[i], _token_tile(h.shape[1], LANES))
            return peer_experts_split(alpha, h, m[:, :, 3], m[:, :, 4], m[:, :, 5], gam[1:2], bet[1:2],
                                      idx_t, gates_t, uv)

        x = peer_block(x, m_l, lat_tile)
        if not last:
            xc = peer_block(xc, m_c, ctx_tile)
    return x
```

```python
import functools

import jax
import jax.numpy as jnp
from jax import lax
from jax.experimental import pallas as pl
from jax.experimental.pallas import tpu as pltpu

F32 = jnp.float32
BF16 = jnp.bfloat16

GRID_W = 64
HG_HEADS = 4
HG_DK = 128
GLA_HEADS = 4
GLA_DK = 64
GLA_DV = 128
GLA_GATE_RANK = 16
GLA_GATE_NORM = 16.0
PEER_HEADS = 8
PEER_NKEYS = 128
PEER_TOPK = 16
EPS = 1e-6
LB_FLOOR = 1e-30

LANES = 128
SUBLANES = 8
VMEM_LIMIT = 56 * 1024 * 1024

_NT = (((1,), (1,)), ((), ()))
_TN = (((0,), (0,)), ((), ()))


def _cparams(sem):
    return pltpu.CompilerParams(dimension_semantics=sem, vmem_limit_bytes=VMEM_LIMIT)


def _layer_norm(x):
    mu = jnp.mean(x, axis=-1, keepdims=True)
    xc = x - mu
    var = jnp.mean(xc * xc, axis=-1, keepdims=True)
    return xc * lax.rsqrt(var + EPS)


def _silu(x):
    return x * jax.nn.sigmoid(x)


def _ada_kernel(c_ref, w_ref, b_ref, o_ref):
    s = _silu(c_ref[...]).astype(BF16)
    o_ref[0] = jnp.dot(s, w_ref[0].astype(BF16), preferred_element_type=F32) + b_ref[0]


def ada_mods(cs, w_ada, b_ada):
    depth, d, n = w_ada.shape
    r = cs.shape[0]
    tn = n // 4
    return pl.pallas_call(
        _ada_kernel,
        grid=(depth, n // tn),
        in_specs=[pl.BlockSpec((r, d), lambda l, j: (0, 0)),
                  pl.BlockSpec((1, d, tn), lambda l, j: (l, 0, j)),
                  pl.BlockSpec((1, 1, tn), lambda l, j: (l, 0, j))],
        out_specs=pl.BlockSpec((1, r, tn), lambda l, j: (l, 0, j)),
        out_shape=jax.ShapeDtypeStruct((depth, r, n), F32),
        compiler_params=_cparams(("parallel", "parallel")),
    )(cs, w_ada, b_ada.reshape(depth, 1, n))


def _ln_mod_matmul_kernel(x_ref, sh_ref, sc_ref, w_ref, o_ref):
    y = _layer_norm(x_ref[0]) * (1.0 + sc_ref[0]) + sh_ref[0]
    o_ref[0] = jnp.dot(y.astype(BF16), w_ref[...], preferred_element_type=F32)


def ln_mod_matmul(x, shift, scale, w_bf16, tile):
    b, l, d = x.shape
    n = w_bf16.shape[1]
    return pl.pallas_call(
        _ln_mod_matmul_kernel,
        grid=(b, l // tile),
        in_specs=[pl.BlockSpec((1, tile, d), lambda i, j: (i, j, 0)),
                  pl.BlockSpec((1, 1, d), lambda i, j: (i, 0, 0)),
                  pl.BlockSpec((1, 1, d), lambda i, j: (i, 0, 0)),
                  pl.BlockSpec((d, n), lambda i, j: (0, 0))],
        out_specs=pl.BlockSpec((1, tile, n), lambda i, j: (i, j, 0)),
        out_shape=jax.ShapeDtypeStruct((b, l, n), F32),
        compiler_params=_cparams(("parallel", "parallel")),
    )(x, shift, scale, w_bf16)


def _chunk_scan(q, k, v, la, st_ref, reverse):
    c, dk = q.shape
    nlev = c.bit_length() - 1
    row = lax.broadcasted_iota(jnp.int32, (c, dk), 0)
    qi = lax.broadcasted_iota(jnp.int32, (c, c), 0)
    ki = lax.broadcasted_iota(jnp.int32, (c, c), 1)
    late, early = (ki, qi) if reverse else (qi, ki)
    scores = jnp.where(qi == ki,
                       lax.dot_general(q.astype(BF16), k.astype(BF16), _NT, preferred_element_type=F32), 0.0)
    p = la
    t = la
    for lev in range(nlev):
        m = 1 << lev
        right = (row & m) != 0
        if reverse:
            d = jnp.where(right, p - la, t - p + la)
        else:
            d = jnp.where(right, p, t - p)
        e = jnp.exp(d)
        s_l = lax.dot_general((q * e).astype(BF16), (k * e).astype(BF16), _NT, preferred_element_type=F32)
        lb = late >> lev
        mask = (lb == (early >> lev) + 1) & ((lb & 1) == 1)
        scores = jnp.where(mask, s_l, scores)
        sib = jnp.where(right, pltpu.roll(t, m, 0), pltpu.roll(t, c - m, 0))
        p = p + jnp.where(right, sib, 0.0)
        t = t + sib
    if reverse:
        dq, dkk = t - p + la, p - la
    else:
        dq, dkk = p, t - p
    st = st_ref[...]
    qs = (q * jnp.exp(dq)).astype(BF16)
    ks = (k * jnp.exp(dkk)).astype(BF16)
    vb = v.astype(BF16)
    o = (jnp.dot(scores.astype(BF16), vb, preferred_element_type=F32)
         + lax.dot_general(qs, st.astype(BF16), _NT, preferred_element_type=F32))
    st_ref[...] = st * jnp.exp(t[0:1, :]) + lax.dot_general(vb, ks, _TN, preferred_element_type=F32)
    return o


def _softplus_neg_abs(d):
    return jnp.log(1.0 + jnp.exp(-jnp.abs(d)))


def _log_sigmoid(z):
    return jnp.minimum(z, 0.0) - _softplus_neg_abs(z)


def _hgrn_gate(z, c0, c1, om):
    ls = _log_sigmoid(z)
    y = c1 + ls
    la = jnp.maximum(c0, y) + _softplus_neg_abs(c0 - y)
    return om * jax.nn.sigmoid(-z), la


def _scan_prologue(s0_ref, st_ref):
    @pl.when(pl.program_id(1) == 0)
    def _():
        st_ref[...] = s0_ref[0]


def _scan_epilogue(sfin_ref, st_ref):
    @pl.when(pl.program_id(1) == pl.num_programs(1) - 1)
    def _():
        sfin_ref[0] = st_ref[...]


def _head(ref, h):
    return ref[0, :, h * LANES:(h + 1) * LANES]


def _hgrn_scan_kernel(reverse, q_ref, v_ref, z_ref, cst_ref, s0_ref, o_ref, sfin_ref, st_ref):
    _scan_prologue(s0_ref, st_ref)
    for h in range(HG_HEADS):
        cols = slice(h * LANES, (h + 1) * LANES)
        k, la = _hgrn_gate(_head(z_ref, h), cst_ref[0:1, cols], cst_ref[1:2, cols], cst_ref[2:3, cols])
        o_ref[0, :, cols] = _chunk_scan(_head(q_ref, h), k, _head(v_ref, h), la, st_ref.at[h], reverse)
    _scan_epilogue(sfin_ref, st_ref)


def _gla_scan_kernel(reverse, q_ref, k_ref, v_ref, r_ref, w_ref, b_ref, s0_ref, o_ref, sfin_ref, st_ref):
    _scan_prologue(s0_ref, st_ref)
    g = jnp.dot(r_ref[0].astype(BF16), w_ref[...].astype(BF16), preferred_element_type=F32) + b_ref[...]
    la_all = _log_sigmoid(g) * (1.0 / GLA_GATE_NORM)
    for h in range(GLA_HEADS):
        cols = slice(h * LANES, (h + 1) * LANES)
        q = _head(q_ref, h) * (GLA_DK ** -0.5)
        o_ref[0, :, cols] = _chunk_scan(q, _head(k_ref, h), _head(v_ref, h), la_all[:, cols], st_ref.at[h], reverse)
    _scan_epilogue(sfin_ref, st_ref)


def _scan_chunk(l):
    return 128 if l % 128 == 0 else 64


def _scan_call(kernel, reverse, batch, l, heads, in_arrays, in_specs, s0):
    c = _scan_chunk(l)
    n = l // c
    width = heads * LANES
    state_spec = pl.BlockSpec((1, heads, LANES, LANES), lambda b, j: (b, 0, 0, 0))
    return pl.pallas_call(
        functools.partial(kernel, reverse),
        grid=(batch, n),
        in_specs=in_specs + [state_spec],
        out_specs=[pl.BlockSpec((1, c, width), lambda b, j: (b, (n - 1 - j) if reverse else j, 0)),
                   state_spec],
        out_shape=[jax.ShapeDtypeStruct((batch, l, width), F32),
                   jax.ShapeDtypeStruct((batch, heads, LANES, LANES), F32)],
        scratch_shapes=[pltpu.VMEM((heads, LANES, LANES), F32)],
        compiler_params=_cparams(("parallel", "arbitrary")),
    )(*in_arrays, s0)


def hgrn_scan(u, gate_block, consts, s0, reverse):
    b, l, _ = u.shape
    c = _scan_chunk(l)
    n = l // c
    width = HG_HEADS * LANES

    def col(block):
        return pl.BlockSpec((1, c, width), lambda i, j: (i, (n - 1 - j) if reverse else j, block))

    in_specs = [col(0), col(3), col(gate_block), pl.BlockSpec((3, width), lambda i, j: (0, 0))]
    return _scan_call(_hgrn_scan_kernel, reverse, b, l, HG_HEADS, (u, u, u, consts), in_specs, s0)


def gla_scan(q, k, v, r, w, bias, s0, reverse):
    b, l, width = q.shape
    c = _scan_chunk(l)
    n = l // c
    rank = r.shape[-1]

    def col(wd):
        return pl.BlockSpec((1, c, wd), lambda i, j: (i, (n - 1 - j) if reverse else j, 0))

    in_specs = [col(width), col(width), col(width), col(rank),
                pl.BlockSpec((rank, width), lambda i, j: (0, 0)),
                pl.BlockSpec((1, width), lambda i, j: (0, 0))]
    return _scan_call(_gla_scan_kernel, reverse, b, l, GLA_HEADS, (q, k, v, r, w, bias), in_specs, s0)


def _head_rms(a, gain, heads):
    outs = []
    for h in range(heads):
        ah = a[:, h * LANES:(h + 1) * LANES]
        outs.append(ah * lax.rsqrt(jnp.mean(ah * ah, axis=-1, keepdims=True) + EPS) * gain)
    return jnp.concatenate(outs, axis=1)


def _mix_out_kernel(alpha, hf_ref, hb_ref, gf_ref, gb_ref, hgate_ref, ggate_ref, x_ref, g1_ref,
                    hgain_ref, ggain_ref, w_ref, gam_ref, bet_ref, o_ref):
    ma = _head_rms(hf_ref[0] + hb_ref[0], hgain_ref[...], HG_HEADS) * _silu(hgate_ref[0])
    mb = _head_rms(gf_ref[0] + gb_ref[0], ggain_ref[...], GLA_HEADS) * _silu(ggate_ref[0])
    mix = jnp.concatenate([ma, mb], axis=1).astype(BF16)
    y = jnp.dot(mix, w_ref[...], preferred_element_type=F32)
    r = alpha * x_ref[0] + g1_ref[0] * y
    o_ref[0] = _layer_norm(r) * gam_ref[...] + bet_ref[...]


def mix_out(alpha, hf, hb, gf, gb, u, x, g1, hgain, ggain, w_out_bf16, gamma, beta, tile):
    b, l, d = x.shape
    hw = HG_HEADS * LANES
    tok = lambda width, blk: pl.BlockSpec((1, tile, width), lambda i, j: (i, j, blk))
    vec = lambda width: pl.BlockSpec((1, width), lambda i, j: (0, 0))
    return pl.pallas_call(
        functools.partial(_mix_out_kernel, alpha),
        grid=(b, l // tile),
        in_specs=[tok(hw, 0), tok(hw, 0), tok(hw, 0), tok(hw, 0),
                  tok(hw, 4), tok(hw, 8), tok(d, 0),
                  pl.BlockSpec((1, 1, d), lambda i, j: (i, 0, 0)),
                  vec(LANES), vec(LANES),
                  pl.BlockSpec(w_out_bf16.shape, lambda i, j: (0, 0)),
                  vec(d), vec(d)],
        out_specs=tok(d, 0),
        out_shape=jax.ShapeDtypeStruct((b, l, d), F32),
        compiler_params=_cparams(("parallel", "parallel")),
    )(hf, hb, gf, gb, u, u, x, g1, hgain, ggain, w_out_bf16, gamma, beta)


_CAND_PIECES = ((0, 16),) + tuple((a, 8) for a in range(1, 8))
_CAND_TAIL = 8
_NCAND = sum(n for _, n in _CAND_PIECES) + _CAND_TAIL


def _extract_top(s, iota, nrows):
    m = jnp.max(s, axis=0, keepdims=True)
    pos = jnp.min(jnp.where(s == m, iota, nrows), axis=0, keepdims=True)
    return m, pos, iota == pos


def _peer_route_kernel(q_ref, k1_ref, k2_ref, idx_ref, g_ref, sc, v1, i1, v2, i2, cs, ci, tv):
    tt = q_ref.shape[1]
    nk, topk = PEER_NKEYS, PEER_TOPK
    for h in range(PEER_HEADS):
        o = 2 * nk * h
        sc[2 * h] = lax.dot_general(k1_ref[...], q_ref[0, :, o:o + nk], _NT,
                                    precision=lax.Precision.HIGHEST, preferred_element_type=F32)
        sc[2 * h + 1] = lax.dot_general(k2_ref[...], q_ref[0, :, o + nk:o + 2 * nk], _NT,
                                        precision=lax.Precision.HIGHEST, preferred_element_type=F32)
    iota = lax.broadcasted_iota(jnp.int32, (nk, tt), 0)
    iota2 = lax.broadcasted_iota(jnp.int32, (_NCAND, tt), 0)

    for h in range(PEER_HEADS + 1):
        first, second = h < PEER_HEADS, h >= 1
        cur, prev = h % 2, (h - 1) % 2
        out0 = (h - 1) * topk
        init = ()
        if first:
            init += (sc[2 * h], sc[2 * h + 1])
        if second:
            row = 0
            for a, n in _CAND_PIECES:
                cs[row:row + n, :] = v1[prev, a:a + 1, :] + v2[prev, 0:n, :]
                ci[row:row + n, :] = i1[prev, a:a + 1, :] * nk + i2[prev, 0:n, :]
                row += n
            cs[row:row + _CAND_TAIL, :] = v1[prev, topk - _CAND_TAIL:topk, :] + v2[prev, 0:1, :]
            ci[row:row + _CAND_TAIL, :] = i1[prev, topk - _CAND_TAIL:topk, :] * nk + i2[prev, 0:1, :]
            init += (cs[...],)

        def step(j, st, first=first, second=second, cur=cur, out0=out0):
            out = ()
            if first:
                s1, s2 = st[0], st[1]
                m1, p1, hit1 = _extract_top(s1, iota, nk)
                m2, p2, hit2 = _extract_top(s2, iota, nk)
                v1[cur, pl.ds(j, 1), :] = m1
                i1[cur, pl.ds(j, 1), :] = p1
                v2[cur, pl.ds(j, 1), :] = m2
                i2[cur, pl.ds(j, 1), :] = p2
                out += (jnp.where(hit1, -jnp.inf, s1), jnp.where(hit2, -jnp.inf, s2))
            if second:
                s = st[-1]
                m, _, hit = _extract_top(s, iota2, _NCAND)
                tv[pl.ds(j, 1), :] = m
                idx_ref[0, pl.ds(out0 + j, 1), :] = jnp.max(jnp.where(hit, ci[...], -1), axis=0, keepdims=True)
                out += (jnp.where(hit, -jnp.inf, s),)
            return out

        lax.fori_loop(0, topk, step, init)
        if second:
            top = tv[...]
            e = jnp.exp(top - top[0:1, :])
            g_ref[0, out0:out0 + topk, :] = e / jnp.sum(e, axis=0, keepdims=True)


def peer_route(q, sub_keys, tile):
    b, l, dq = q.shape
    nslot = PEER_HEADS * PEER_TOPK
    out_spec = pl.BlockSpec((1, nslot, tile), lambda i, j: (i, 0, j))
    keys = pl.BlockSpec((PEER_NKEYS, PEER_NKEYS), lambda i, j: (0, 0))
    return pl.pallas_call(
        _peer_route_kernel,
        grid=(b, l // tile),
        in_specs=[pl.BlockSpec((1, tile, dq), lambda i, j: (i, j, 0)), keys, keys],
        out_specs=[out_spec, out_spec],
        out_shape=[jax.ShapeDtypeStruct((b, nslot, l), jnp.int32),
                   jax.ShapeDtypeStruct((b, nslot, l), F32)],
        scratch_shapes=[pltpu.VMEM((2 * PEER_HEADS, PEER_NKEYS, tile), F32),
                        pltpu.VMEM((2, PEER_TOPK, tile), F32), pltpu.VMEM((2, PEER_TOPK, tile), jnp.int32),
                        pltpu.VMEM((2, PEER_TOPK, tile), F32), pltpu.VMEM((2, PEER_TOPK, tile), jnp.int32),
                        pltpu.VMEM((_NCAND, tile), F32), pltpu.VMEM((_NCAND, tile), jnp.int32),
                        pltpu.VMEM((PEER_TOPK, tile), F32)],
        compiler_params=_cparams(("parallel", "parallel")),
    )(q, sub_keys[0], sub_keys[1])


PEER_TOK_TILE = 8
PEER_ISSUE_SLACK = 2
ROW_WORDS = SUBLANES
PAIR_ROWS = ROW_WORDS // 2


def _unpack_pairs(words):
    lo = pltpu.bitcast(words << 16, F32)
    hi = pltpu.bitcast(words & jnp.int32(-65536), F32)
    return lo, hi


def _expert_token(words, hm, g_ref, y_ref, tok):
    hrow = lambda c: hm[tok:tok + 1, c * LANES:(c + 1) * LANES]
    acc = None
    for j in range(PAIR_ROWS):
        lo, hi = _unpack_pairs(words(j))
        part = lo * hrow(2 * j) + hi * hrow(2 * j + 1)
        acc = part if acc is None else acc + part
    a = jnp.sum(acc, axis=1, keepdims=True)
    w = jax.nn.gelu(a) * g_ref[0, :, tok:tok + 1]
    for j in range(PAIR_ROWS):
        lo, hi = _unpack_pairs(words(PAIR_ROWS + j))
        for c, vc in ((2 * j, lo), (2 * j + 1, hi)):
            y_ref[tok:tok + 1, c * LANES:(c + 1) * LANES] = jnp.sum(vc * w, axis=0, keepdims=True)


def _peer_experts_kernel(alpha, ids_ref, ids_next_ref, g_ref, x_ref, sh_ref, sc_ref, g2_ref, gam_ref, bet_ref,
                         uv_hbm, o_ref, buf_a, buf_b, y_ref, sem):
    tt = PEER_TOK_TILE
    d = x_ref.shape[1]
    nslot = g_ref.shape[1]
    step = pl.program_id(0)
    last = pl.num_programs(0) - 1
    bufs = (buf_a, buf_b)

    def row_copy(expert, dst, row, s):
        return pltpu.make_async_copy(uv_hbm.at[expert], dst.at[pl.ds(row, ROW_WORDS)], sem.at[s])

    def tile_wait(buf, s):
        pltpu.make_async_copy(buf, buf, sem.at[s]).wait()

    @pl.when(step == 0)
    def _():
        def per_token(t, carry):
            def per_group(g, carry):
                for j in range(SUBLANES):
                    k = g * SUBLANES + j
                    row = pl.multiple_of((t * nslot + k) * ROW_WORDS, ROW_WORDS)
                    row_copy(ids_ref[0, t, k], buf_a, row, 0).start(priority=j % 2)
                return carry
            return lax.fori_loop(0, nslot // SUBLANES, per_group, carry)
        lax.fori_loop(0, tt, per_token, 0)

    x = x_ref[...]
    hm = _layer_norm(x) * (1.0 + sc_ref[0]) + sh_ref[0]

    issue_tokens = tt - PEER_ISSUE_SLACK
    per_token_issue = -(-tt * nslot // issue_tokens)

    for phase in range(2):
        cur, nxt = bufs[phase], bufs[1 - phase]
        tile_wait(cur, phase)
        for t in range(tt):
            for n in range(t * per_token_issue, min((t + 1) * per_token_issue, tt * nslot)):
                tn, k = divmod(n, nslot)
                expert = ids_ref[0, tt + tn, k] if phase == 0 else ids_next_ref[0, tn, k]
                row_copy(expert, nxt, n * ROW_WORDS, 1 - phase).start(priority=n % 2)
            tok = phase * tt + t
            base = t * nslot * ROW_WORDS
            words = lambda j, base=base, cur=cur: cur[pl.ds(base + j, nslot, stride=ROW_WORDS), :]
            _expert_token(words, hm, g_ref, y_ref, tok)

    r = alpha * x + g2_ref[0] * y_ref[...]
    o_ref[...] = _layer_norm(r) * gam_ref[...] + bet_ref[...]

    @pl.when(step == last)
    def _():
        tile_wait(buf_a, 0)


def pack_experts(u, v):
    def pack(a):
        bits = lax.bitcast_convert_type(a.astype(BF16).reshape(a.shape[0], PAIR_ROWS, 2, LANES), jnp.uint16)
        bits = bits.astype(jnp.uint32)
        return bits[:, :, 0, :] | (bits[:, :, 1, :] << 16)
    return lax.bitcast_convert_type(jnp.concatenate([pack(u), pack(v)], axis=1), jnp.int32)


def peer_experts(alpha, x, shift, scale, g2, gamma, beta, idx_t, gates_t, uv):
    b, l, d = x.shape
    assert d == 2 * PAIR_ROWS * LANES
    nslot = idx_t.shape[1]
    blk = 2 * PEER_TOK_TILE
    blocks_per_batch = l // blk
    nb = b * blocks_per_batch
    idx = idx_t.transpose(0, 2, 1).reshape(nb, blk, nslot)
    gates = gates_t.reshape(b, nslot, blocks_per_batch, blk).transpose(0, 2, 1, 3).reshape(nb, nslot, blk)
    per_batch = lambda i: (i // blocks_per_batch, 0, 0)
    vec = pl.BlockSpec((1, d), lambda i: (0, 0))
    gather_buf = pltpu.VMEM((PEER_TOK_TILE * nslot * ROW_WORDS, LANES), jnp.int32)
    out = pl.pallas_call(
        functools.partial(_peer_experts_kernel, alpha),
        grid=(nb,),
        in_specs=[pl.BlockSpec((1, blk, nslot), lambda i: (i, 0, 0), memory_space=pltpu.SMEM),
                  pl.BlockSpec((1, blk, nslot), lambda i: (jnp.minimum(i + 1, nb - 1), 0, 0),
                               memory_space=pltpu.SMEM),
                  pl.BlockSpec((1, nslot, blk), lambda i: (i, 0, 0)),
                  pl.BlockSpec((blk, d), lambda i: (i, 0)),
                  pl.BlockSpec((1, 1, d), per_batch),
                  pl.BlockSpec((1, 1, d), per_batch),
                  pl.BlockSpec((1, 1, d), per_batch),
                  vec, vec,
                  pl.BlockSpec(memory_space=pl.ANY)],
        out_specs=pl.BlockSpec((blk, d), lambda i: (i, 0)),
        out_shape=jax.ShapeDtypeStruct((b * l, d), F32),
        scratch_shapes=[gather_buf, gather_buf, pltpu.VMEM((blk, d), F32), pltpu.SemaphoreType.DMA((2,))],
        compiler_params=_cparams(("arbitrary",)),
    )(idx, idx, gates, x.reshape(b * l, d), shift, scale, g2, gamma, beta, uv)
    return out.reshape(b, l, d)


def _to_col_major(t, rows):
    b, l, c = t.shape
    return t.reshape(b, rows, GRID_W, c).transpose(0, 2, 1, 3).reshape(b, l, c)


def _from_col_major(t, rows):
    b, l, c = t.shape
    return t.reshape(b, GRID_W, rows, c).transpose(0, 2, 1, 3).reshape(b, l, c)


def _pad_heads(t):
    b, l, _ = t.shape
    t = t.reshape(b, l, GLA_HEADS, GLA_DK)
    t = jnp.pad(t, ((0, 0), (0, 0), (0, 0), (0, LANES - GLA_DK)))
    return t.reshape(b, l, GLA_HEADS * LANES)


_HG_WIDTH = 5 * HG_HEADS * LANES
_GLA_WIDTH = GLA_HEADS * LANES


def _project_weights(w):
    o, kw = _HG_WIDTH, GLA_HEADS * GLA_DK
    pad_cols = lambda cols: _pad_heads(cols[None])[0]
    w = jnp.concatenate([w[:, :o], pad_cols(w[:, o:o + kw]), pad_cols(w[:, o + kw:o + 2 * kw]),
                         w[:, o + 2 * kw:]], axis=1)
    return jnp.pad(w, ((0, 0), (0, -w.shape[1] % LANES))).astype(BF16)


def _gla_parts(u, rows):
    o, gw = _HG_WIDTH, _GLA_WIDTH
    order = (lambda t: t) if rows is None else (lambda t: _to_col_major(t, rows))
    q, k, v = (order(u[..., o + n * gw:o + (n + 1) * gw]) for n in range(3))
    r0 = o + 4 * gw
    rf = order(u[..., r0:r0 + GLA_GATE_RANK])
    rb = order(u[..., r0 + GLA_GATE_RANK:r0 + 2 * GLA_GATE_RANK])
    return q, k, v, rf, rb


def _token_tile(l, want):
    while l % want:
        want //= 2
    return want


def kernel(x, c, ctx, c_ctx, w_ada, b_ada, w_in, w_gk2, b_gk, hg_lower_bounds, hg_norm, gla_norm, w_out,
           ln_gamma, ln_beta, peer_w_query, peer_sub_keys, peer_u, peer_v):
    batch, seq, d = x.shape
    depth = w_ada.shape[0]
    rows = seq // GRID_W
    alpha = (2.0 * depth) ** 0.25

    sm = jax.nn.softmax(hg_lower_bounds.astype(F32), axis=1)
    lb_all = jnp.clip(jnp.cumsum(sm, axis=1) - sm[:, :1], 0.0, 1.0 - 1e-6)
    hg_consts = jnp.stack([jnp.log(jnp.maximum(lb_all, LB_FLOOR)), jnp.log1p(-lb_all), 1.0 - lb_all], axis=2)

    cs = jnp.concatenate([c, c_ctx[None, :], jnp.zeros((SUBLANES - 1 - batch % SUBLANES, d), F32)], axis=0)
    mods = ada_mods(cs, w_ada, b_ada)

    zero_state = jnp.zeros((batch, HG_HEADS, LANES, LANES), F32)
    lat_tile = _token_tile(seq, 256)
    ctx_tile = _token_tile(ctx.shape[1], 256)
    gw_pad = lambda w: _pad_heads(w[None])[0]

    xc = ctx
    for i in range(depth):
        last = i == depth - 1
        m_l = mods[i, :batch].reshape(batch, 1, 6, d)
        m_c = jnp.broadcast_to(mods[i, batch].reshape(1, 1, 6, d), (batch, 1, 6, d))
        w_in_b = _project_weights(w_in[i])
        w_out_b = w_out[i].astype(BF16)
        w_q_b = peer_w_query[i].astype(BF16)
        gam, bet = ln_gamma[i], ln_beta[i]
        hgain, ggain = hg_norm[i][None, :], gla_norm[i][None, :]
        gk_w = [gw_pad(w_gk2[i, dd]) for dd in range(2)]
        gk_b = [gw_pad(b_gk[i, dd][None, :]) for dd in range(2)]
        uv = pack_experts(peer_u[i], peer_v[i])

        u_c = ln_mod_matmul(xc, m_c[:, :, 0], m_c[:, :, 1], w_in_b, ctx_tile)
        u_l = ln_mod_matmul(x, m_l[:, :, 0], m_l[:, :, 1], w_in_b, lat_tile)
        h_out, g_out = {}, {}
        gc = _gla_parts(u_c, None)
        gl = _gla_parts(u_l, rows)
        for dd, rev in ((0, False), (1, True)):
            o_c, s_c = hgrn_scan(u_c, 1 + dd, hg_consts[dd, i], zero_state, rev)
            o_l, _ = hgrn_scan(u_l, 1 + dd, hg_consts[dd, i], s_c, rev)
            h_out[dd] = (o_c, o_l)
            o_c, s_c = gla_scan(gc[0], gc[1], gc[2], gc[3 + dd], gk_w[dd], gk_b[dd], zero_state, rev)
            o_l, _ = gla_scan(gl[0], gl[1], gl[2], gl[3 + dd], gk_w[dd], gk_b[dd], s_c, rev)
            g_out[dd] = (o_c, _from_col_major(o_l, rows))
        x = mix_out(alpha, h_out[0][1], h_out[1][1], g_out[0][1], g_out[1][1], u_l, x, m_l[:, :, 2],
                    hgain, ggain, w_out_b, gam[0:1], bet[0:1], lat_tile)
        if not last:
            xc = mix_out(alpha, h_out[0][0], h_out[1][0], g_out[0][0], g_out[1][0], u_c, xc, m_c[:, :, 2],
                         hgain, ggain, w_out_b, gam[0:1], bet[0:1], ctx_tile)

        def peer_block(h, m, tile):
            qh = ln_mod_matmul(h, m[:, :, 3], m[:, :, 4], w_q_b, tile)
            idx_t, gates_t = peer_route(qh, peer_sub_keys[i], _token_tile(h.shape[1], LANES))
            return peer_experts(alpha, h, m[:, :, 3], m[:, :, 4], m[:, :, 5], gam[1:2], bet[1:2], idx_t, gates_t, uv)

        x = peer_block(x, m_l, lat_tile)
        if not last:
            xc = peer_block(xc, m_c, ctx_tile)
    return x
```

```python
import functools

import jax
import jax.numpy as jnp
from jax import lax
from jax.experimental import pallas as pl
from jax.experimental.pallas import tpu as pltpu
from jax.experimental.pallas import tpu_sc as plsc

F32 = jnp.float32
BF16 = jnp.bfloat16

GRID_W = 64
HG_HEADS = 4
HG_DK = 128
GLA_HEADS = 4
GLA_DK = 64
GLA_DV = 128
GLA_GATE_RANK = 16
GLA_GATE_NORM = 16.0
PEER_HEADS = 8
PEER_NKEYS = 128
PEER_TOPK = 16
EPS = 1e-6
LB_FLOOR = 1e-30

LANES = 128
SUBLANES = 8
VMEM_LIMIT = 56 * 1024 * 1024

_NT = (((1,), (1,)), ((), ()))
_TN = (((0,), (0,)), ((), ()))


def _cparams(sem):
    return pltpu.CompilerParams(dimension_semantics=sem, vmem_limit_bytes=VMEM_LIMIT)


def _layer_norm(x):
    mu = jnp.mean(x, axis=-1, keepdims=True)
    xc = x - mu
    var = jnp.mean(xc * xc, axis=-1, keepdims=True)
    return xc * lax.rsqrt(var + EPS)


def _silu(x):
    return x * jax.nn.sigmoid(x)


def _ada_kernel(c_ref, w_ref, b_ref, o_ref):
    s = _silu(c_ref[...]).astype(BF16)
    o_ref[0] = jnp.dot(s, w_ref[0].astype(BF16), preferred_element_type=F32) + b_ref[0]


def ada_mods(cs, w_ada, b_ada):
    depth, d, n = w_ada.shape
    r = cs.shape[0]
    tn = n // 4
    return pl.pallas_call(
        _ada_kernel,
        grid=(depth, n // tn),
        in_specs=[pl.BlockSpec((r, d), lambda l, j: (0, 0)),
                  pl.BlockSpec((1, d, tn), lambda l, j: (l, 0, j)),
                  pl.BlockSpec((1, 1, tn), lambda l, j: (l, 0, j))],
        out_specs=pl.BlockSpec((1, r, tn), lambda l, j: (l, 0, j)),
        out_shape=jax.ShapeDtypeStruct((depth, r, n), F32),
        compiler_params=_cparams(("parallel", "parallel")),
    )(cs, w_ada, b_ada.reshape(depth, 1, n))


def _ln_mod_matmul_kernel(x_ref, sh_ref, sc_ref, w_ref, o_ref):
    y = _layer_norm(x_ref[0]) * (1.0 + sc_ref[0]) + sh_ref[0]
    o_ref[0] = jnp.dot(y.astype(BF16), w_ref[...], preferred_element_type=F32)


def ln_mod_matmul(x, shift, scale, w_bf16, tile):
    b, l, d = x.shape
    n = w_bf16.shape[1]
    return pl.pallas_call(
        _ln_mod_matmul_kernel,
        grid=(b, l // tile),
        in_specs=[pl.BlockSpec((1, tile, d), lambda i, j: (i, j, 0)),
                  pl.BlockSpec((1, 1, d), lambda i, j: (i, 0, 0)),
                  pl.BlockSpec((1, 1, d), lambda i, j: (i, 0, 0)),
                  pl.BlockSpec((d, n), lambda i, j: (0, 0))],
        out_specs=pl.BlockSpec((1, tile, n), lambda i, j: (i, j, 0)),
        out_shape=jax.ShapeDtypeStruct((b, l, n), F32),
        compiler_params=_cparams(("parallel", "parallel")),
    )(x, shift, scale, w_bf16)


def _chunk_scan(q, k, v, la, st_ref, reverse):
    c, dk = q.shape
    nlev = c.bit_length() - 1
    row = lax.broadcasted_iota(jnp.int32, (c, dk), 0)
    qi = lax.broadcasted_iota(jnp.int32, (c, c), 0)
    ki = lax.broadcasted_iota(jnp.int32, (c, c), 1)
    late, early = (ki, qi) if reverse else (qi, ki)
    scores = jnp.where(qi == ki,
                       lax.dot_general(q.astype(BF16), k.astype(BF16), _NT, preferred_element_type=F32), 0.0)
    p = la
    t = la
    for lev in range(nlev):
        m = 1 << lev
        right = (row & m) != 0
        if reverse:
            d = jnp.where(right, p - la, t - p + la)
        else:
            d = jnp.where(right, p, t - p)
        e = jnp.exp(d)
        s_l = lax.dot_general((q * e).astype(BF16), (k * e).astype(BF16), _NT, preferred_element_type=F32)
        lb = late >> lev
        mask = (lb == (early >> lev) + 1) & ((lb & 1) == 1)
        scores = jnp.where(mask, s_l, scores)
        sib = jnp.where(right, pltpu.roll(t, m, 0), pltpu.roll(t, c - m, 0))
        p = p + jnp.where(right, sib, 0.0)
        t = t + sib
    if reverse:
        dq, dkk = t - p + la, p - la
    else:
        dq, dkk = p, t - p
    st = st_ref[...]
    qs = (q * jnp.exp(dq)).astype(BF16)
    ks = (k * jnp.exp(dkk)).astype(BF16)
    vb = v.astype(BF16)
    o = (jnp.dot(scores.astype(BF16), vb, preferred_element_type=F32)
         + lax.dot_general(qs, st.astype(BF16), _NT, preferred_element_type=F32))
    st_ref[...] = st * jnp.exp(t[0:1, :]) + lax.dot_general(vb, ks, _TN, preferred_element_type=F32)
    return o


def _softplus_neg_abs(d):
    return jnp.log(1.0 + jnp.exp(-jnp.abs(d)))


def _log_sigmoid(z):
    return jnp.minimum(z, 0.0) - _softplus_neg_abs(z)


def _hgrn_gate(z, c0, c1, om):
    ls = _log_sigmoid(z)
    y = c1 + ls
    la = jnp.maximum(c0, y) + _softplus_neg_abs(c0 - y)
    return om * jax.nn.sigmoid(-z), la


def _scan_prologue(s0_ref, st_ref):
    @pl.when(pl.program_id(1) == 0)
    def _():
        st_ref[...] = s0_ref[0]


def _scan_epilogue(sfin_ref, st_ref):
    @pl.when(pl.program_id(1) == pl.num_programs(1) - 1)
    def _():
        sfin_ref[0] = st_ref[...]


def _head(ref, h):
    return ref[0, :, h * LANES:(h + 1) * LANES]


def _hgrn_scan_kernel(reverse, q_ref, v_ref, z_ref, cst_ref, s0_ref, o_ref, sfin_ref, st_ref):
    _scan_prologue(s0_ref, st_ref)
    for h in range(HG_HEADS):
        cols = slice(h * LANES, (h + 1) * LANES)
        k, la = _hgrn_gate(_head(z_ref, h), cst_ref[0:1, cols], cst_ref[1:2, cols], cst_ref[2:3, cols])
        o_ref[0, :, cols] = _chunk_scan(_head(q_ref, h), k, _head(v_ref, h), la, st_ref.at[h], reverse)
    _scan_epilogue(sfin_ref, st_ref)


def _gla_scan_kernel(reverse, q_ref, k_ref, v_ref, r_ref, w_ref, b_ref, s0_ref, o_ref, sfin_ref, st_ref):
    _scan_prologue(s0_ref, st_ref)
    g = jnp.dot(r_ref[0].astype(BF16), w_ref[...].astype(BF16), preferred_element_type=F32) + b_ref[...]
    la_all = _log_sigmoid(g) * (1.0 / GLA_GATE_NORM)
    for h in range(GLA_HEADS):
        cols = slice(h * LANES, (h + 1) * LANES)
        q = _head(q_ref, h) * (GLA_DK ** -0.5)
        o_ref[0, :, cols] = _chunk_scan(q, _head(k_ref, h), _head(v_ref, h), la_all[:, cols], st_ref.at[h], reverse)
    _scan_epilogue(sfin_ref, st_ref)


def _scan_chunk(l):
    return 128 if l % 128 == 0 else 64


def _scan_call(kernel, reverse, batch, l, heads, in_arrays, in_specs, s0):
    c = _scan_chunk(l)
    n = l // c
    width = heads * LANES
    state_spec = pl.BlockSpec((1, heads, LANES, LANES), lambda b, j: (b, 0, 0, 0))
    return pl.pallas_call(
        functools.partial(kernel, reverse),
        grid=(batch, n),
        in_specs=in_specs + [state_spec],
        out_specs=[pl.BlockSpec((1, c, width), lambda b, j: (b, (n - 1 - j) if reverse else j, 0)),
                   state_spec],
        out_shape=[jax.ShapeDtypeStruct((batch, l, width), F32),
                   jax.ShapeDtypeStruct((batch, heads, LANES, LANES), F32)],
        scratch_shapes=[pltpu.VMEM((heads, LANES, LANES), F32)],
        compiler_params=_cparams(("parallel", "arbitrary")),
    )(*in_arrays, s0)


def hgrn_scan(u, gate_block, consts, s0, reverse):
    b, l, _ = u.shape
    c = _scan_chunk(l)
    n = l // c
    width = HG_HEADS * LANES

    def col(block):
        return pl.BlockSpec((1, c, width), lambda i, j: (i, (n - 1 - j) if reverse else j, block))

    in_specs = [col(0), col(3), col(gate_block), pl.BlockSpec((3, width), lambda i, j: (0, 0))]
    return _scan_call(_hgrn_scan_kernel, reverse, b, l, HG_HEADS, (u, u, u, consts), in_specs, s0)


def gla_scan(q, k, v, r, w, bias, s0, reverse):
    b, l, width = q.shape
    c = _scan_chunk(l)
    n = l // c
    rank = r.shape[-1]

    def col(wd):
        return pl.BlockSpec((1, c, wd), lambda i, j: (i, (n - 1 - j) if reverse else j, 0))

    in_specs = [col(width), col(width), col(width), col(rank),
                pl.BlockSpec((rank, width), lambda i, j: (0, 0)),
                pl.BlockSpec((1, width), lambda i, j: (0, 0))]
    return _scan_call(_gla_scan_kernel, reverse, b, l, GLA_HEADS, (q, k, v, r, w, bias), in_specs, s0)


def _head_rms(a, gain, heads):
    outs = []
    for h in range(heads):
        ah = a[:, h * LANES:(h + 1) * LANES]
        outs.append(ah * lax.rsqrt(jnp.mean(ah * ah, axis=-1, keepdims=True) + EPS) * gain)
    return jnp.concatenate(outs, axis=1)


def _mix_out_kernel(alpha, hf_ref, hb_ref, gf_ref, gb_ref, hgate_ref, ggate_ref, x_ref, g1_ref,
                    hgain_ref, ggain_ref, w_ref, gam_ref, bet_ref, o_ref):
    ma = _head_rms(hf_ref[0] + hb_ref[0], hgain_ref[...], HG_HEADS) * _silu(hgate_ref[0])
    mb = _head_rms(gf_ref[0] + gb_ref[0], ggain_ref[...], GLA_HEADS) * _silu(ggate_ref[0])
    mix = jnp.concatenate([ma, mb], axis=1).astype(BF16)
    y = jnp.dot(mix, w_ref[...], preferred_element_type=F32)
    r = alpha * x_ref[0] + g1_ref[0] * y
    o_ref[0] = _layer_norm(r) * gam_ref[...] + bet_ref[...]


def mix_out(alpha, hf, hb, gf, gb, u, x, g1, hgain, ggain, w_out_bf16, gamma, beta, tile):
    b, l, d = x.shape
    hw = HG_HEADS * LANES
    tok = lambda width, blk: pl.BlockSpec((1, tile, width), lambda i, j: (i, j, blk))
    vec = lambda width: pl.BlockSpec((1, width), lambda i, j: (0, 0))
    return pl.pallas_call(
        functools.partial(_mix_out_kernel, alpha),
        grid=(b, l // tile),
        in_specs=[tok(hw, 0), tok(hw, 0), tok(hw, 0), tok(hw, 0),
                  tok(hw, 4), tok(hw, 8), tok(d, 0),
                  pl.BlockSpec((1, 1, d), lambda i, j: (i, 0, 0)),
                  vec(LANES), vec(LANES),
                  pl.BlockSpec(w_out_bf16.shape, lambda i, j: (0, 0)),
                  vec(d), vec(d)],
        out_specs=tok(d, 0),
        out_shape=jax.ShapeDtypeStruct((b, l, d), F32),
        compiler_params=_cparams(("parallel", "parallel")),
    )(hf, hb, gf, gb, u, u, x, g1, hgain, ggain, w_out_bf16, gamma, beta)


_CAND_PIECES = ((0, 16),) + tuple((a, 8) for a in range(1, 8))
_CAND_TAIL = 8
_NCAND = sum(n for _, n in _CAND_PIECES) + _CAND_TAIL


def _extract_top(s, iota, nrows):
    m = jnp.max(s, axis=0, keepdims=True)
    pos = jnp.min(jnp.where(s == m, iota, nrows), axis=0, keepdims=True)
    return m, pos, iota == pos


def _peer_route_kernel(q_ref, k1_ref, k2_ref, idx_ref, g_ref, sc, v1, i1, v2, i2, cs, ci, tv):
    tt = q_ref.shape[1]
    nk, topk = PEER_NKEYS, PEER_TOPK
    for h in range(PEER_HEADS):
        o = 2 * nk * h
        sc[2 * h] = lax.dot_general(k1_ref[...], q_ref[0, :, o:o + nk], _NT,
                                    precision=lax.Precision.HIGHEST, preferred_element_type=F32)
        sc[2 * h + 1] = lax.dot_general(k2_ref[...], q_ref[0, :, o + nk:o + 2 * nk], _NT,
                                        precision=lax.Precision.HIGHEST, preferred_element_type=F32)
    iota = lax.broadcasted_iota(jnp.int32, (nk, tt), 0)
    iota2 = lax.broadcasted_iota(jnp.int32, (_NCAND, tt), 0)

    for h in range(PEER_HEADS + 1):
        first, second = h < PEER_HEADS, h >= 1
        cur, prev = h % 2, (h - 1) % 2
        out0 = (h - 1) * topk
        init = ()
        if first:
            init += (sc[2 * h], sc[2 * h + 1])
        if second:
            row = 0
            for a, n in _CAND_PIECES:
                cs[row:row + n, :] = v1[prev, a:a + 1, :] + v2[prev, 0:n, :]
                ci[row:row + n, :] = i1[prev, a:a + 1, :] * nk + i2[prev, 0:n, :]
                row += n
            cs[row:row + _CAND_TAIL, :] = v1[prev, topk - _CAND_TAIL:topk, :] + v2[prev, 0:1, :]
            ci[row:row + _CAND_TAIL, :] = i1[prev, topk - _CAND_TAIL:topk, :] * nk + i2[prev, 0:1, :]
            init += (cs[...],)

        def step(j, st, first=first, second=second, cur=cur, out0=out0):
            out = ()
            if first:
                s1, s2 = st[0], st[1]
                m1, p1, hit1 = _extract_top(s1, iota, nk)
                m2, p2, hit2 = _extract_top(s2, iota, nk)
                v1[cur, pl.ds(j, 1), :] = m1
                i1[cur, pl.ds(j, 1), :] = p1
                v2[cur, pl.ds(j, 1), :] = m2
                i2[cur, pl.ds(j, 1), :] = p2
                out += (jnp.where(hit1, -jnp.inf, s1), jnp.where(hit2, -jnp.inf, s2))
            if second:
                s = st[-1]
                m, _, hit = _extract_top(s, iota2, _NCAND)
                tv[pl.ds(j, 1), :] = m
                idx_ref[0, pl.ds(out0 + j, 1), :] = jnp.max(jnp.where(hit, ci[...], -1), axis=0, keepdims=True)
                out += (jnp.where(hit, -jnp.inf, s),)
            return out

        lax.fori_loop(0, topk, step, init)
        if second:
            top = tv[...]
            e = jnp.exp(top - top[0:1, :])
            g_ref[0, out0:out0 + topk, :] = e / jnp.sum(e, axis=0, keepdims=True)


def peer_route(q, sub_keys, tile):
    b, l, dq = q.shape
    nslot = PEER_HEADS * PEER_TOPK
    out_spec = pl.BlockSpec((1, nslot, tile), lambda i, j: (i, 0, j))
    keys = pl.BlockSpec((PEER_NKEYS, PEER_NKEYS), lambda i, j: (0, 0))
    return pl.pallas_call(
        _peer_route_kernel,
        grid=(b, l // tile),
        in_specs=[pl.BlockSpec((1, tile, dq), lambda i, j: (i, j, 0)), keys, keys],
        out_specs=[out_spec, out_spec],
        out_shape=[jax.ShapeDtypeStruct((b, nslot, l), jnp.int32),
                   jax.ShapeDtypeStruct((b, nslot, l), F32)],
        scratch_shapes=[pltpu.VMEM((2 * PEER_HEADS, PEER_NKEYS, tile), F32),
                        pltpu.VMEM((2, PEER_TOPK, tile), F32), pltpu.VMEM((2, PEER_TOPK, tile), jnp.int32),
                        pltpu.VMEM((2, PEER_TOPK, tile), F32), pltpu.VMEM((2, PEER_TOPK, tile), jnp.int32),
                        pltpu.VMEM((_NCAND, tile), F32), pltpu.VMEM((_NCAND, tile), jnp.int32),
                        pltpu.VMEM((PEER_TOPK, tile), F32)],
        compiler_params=_cparams(("parallel", "parallel")),
    )(q, sub_keys[0], sub_keys[1])


PEER_TOK_TILE = 8
PEER_ISSUE_SLACK = 2
ROW_WORDS = SUBLANES
PAIR_ROWS = ROW_WORDS // 2


def _unpack_pairs(words):
    lo = pltpu.bitcast(words << 16, F32)
    hi = pltpu.bitcast(words & jnp.int32(-65536), F32)
    return lo, hi


def _expert_token(pieces, hm, y_ref, tok):
    hrow = lambda c: hm[tok:tok + 1, c * LANES:(c + 1) * LANES]
    weights = []
    for words, gate in pieces:
        acc = None
        for j in range(PAIR_ROWS):
            lo, hi = _unpack_pairs(words(j))
            part = lo * hrow(2 * j) + hi * hrow(2 * j + 1)
            acc = part if acc is None else acc + part
        weights.append(jax.nn.gelu(jnp.sum(acc, axis=1, keepdims=True)) * gate)
    for j in range(PAIR_ROWS):
        halves = [_unpack_pairs(words(PAIR_ROWS + j)) for words, _ in pieces]
        for c, half in ((2 * j, 0), (2 * j + 1, 1)):
            y_ref[tok:tok + 1, c * LANES:(c + 1) * LANES] = sum(
                jnp.sum(h[half] * w, axis=0, keepdims=True) for h, w in zip(halves, weights))


def _peer_experts_kernel(alpha, ids_ref, ids_next_ref, g_ref, x_ref, sh_ref, sc_ref, g2_ref, gam_ref, bet_ref,
                         uv_hbm, *rest):
    pre_ref = rest[0] if len(rest) == 6 else None
    o_ref, buf_a, buf_b, y_ref, sem = rest[-5:]
    tt = PEER_TOK_TILE
    nslot = ids_ref.shape[2]
    npre = g_ref.shape[1] - nslot
    step = pl.program_id(0)
    last = pl.num_programs(0) - 1
    bufs = (buf_a, buf_b)

    def row_copy(expert, dst, row, s):
        return pltpu.make_async_copy(uv_hbm.at[expert], dst.at[pl.ds(row, ROW_WORDS)], sem.at[s])

    def tile_wait(buf, s):
        pltpu.make_async_copy(buf, buf, sem.at[s]).wait()

    @pl.when(step == 0)
    def _():
        def per_token(t, carry):
            def per_group(g, carry):
                for j in range(SUBLANES):
                    k = g * SUBLANES + j
                    row = pl.multiple_of((t * nslot + k) * ROW_WORDS, ROW_WORDS)
                    row_copy(ids_ref[0, t, k], buf_a, row, 0).start(priority=j % 2)
                return carry
            return lax.fori_loop(0, nslot // SUBLANES, per_group, carry)
        lax.fori_loop(0, tt, per_token, 0)

    x = x_ref[...]
    hm = _layer_norm(x) * (1.0 + sc_ref[0]) + sh_ref[0]

    issue_tokens = tt - PEER_ISSUE_SLACK
    per_token_issue = -(-tt * nslot // issue_tokens)

    for phase in range(2):
        cur, nxt = bufs[phase], bufs[1 - phase]
        tile_wait(cur, phase)
        for t in range(tt):
            for n in range(t * per_token_issue, min((t + 1) * per_token_issue, tt * nslot)):
                tn, k = divmod(n, nslot)
                expert = ids_ref[0, tt + tn, k] if phase == 0 else ids_next_ref[0, tn, k]
                row_copy(expert, nxt, n * ROW_WORDS, 1 - phase).start(priority=n % 2)
            tok = phase * tt + t
            base = t * nslot * ROW_WORDS
            words = lambda j, base=base, cur=cur: cur[pl.ds(base + j, nslot, stride=ROW_WORDS), :]
            pieces = [(words, g_ref[0, 0:nslot, tok:tok + 1])]
            if pre_ref is not None:
                pre = lambda j, tok=tok: pre_ref[tok * npre:(tok + 1) * npre, j * LANES:(j + 1) * LANES]
                pieces.append((pre, g_ref[0, nslot:nslot + npre, tok:tok + 1]))
            _expert_token(pieces, hm, y_ref, tok)

    r = alpha * x + g2_ref[0] * y_ref[...]
    o_ref[...] = _layer_norm(r) * gam_ref[...] + bet_ref[...]

    @pl.when(step == last)
    def _():
        tile_wait(buf_a, 0)


def pack_experts(u, v):
    def pack(a):
        bits = lax.bitcast_convert_type(a.astype(BF16).reshape(a.shape[0], PAIR_ROWS, 2, LANES), jnp.uint16)
        bits = bits.astype(jnp.uint32)
        return bits[:, :, 0, :] | (bits[:, :, 1, :] << 16)
    return lax.bitcast_convert_type(jnp.concatenate([pack(u), pack(v)], axis=1), jnp.int32)


def peer_experts(alpha, x, shift, scale, g2, gamma, beta, idx_t, gates_t, uv, pre_rows=None):
    b, l, d = x.shape
    assert d == 2 * PAIR_ROWS * LANES
    nslot, nall = idx_t.shape[1], gates_t.shape[1]
    blk = 2 * PEER_TOK_TILE
    blocks_per_batch = l // blk
    nb = b * blocks_per_batch
    idx = idx_t.transpose(0, 2, 1).reshape(nb, blk, nslot)
    gates = gates_t.reshape(b, nall, blocks_per_batch, blk).transpose(0, 2, 1, 3).reshape(nb, nall, blk)
    per_batch = lambda i: (i // blocks_per_batch, 0, 0)
    vec = pl.BlockSpec((1, d), lambda i: (0, 0))
    gather_buf = pltpu.VMEM((PEER_TOK_TILE * nslot * ROW_WORDS, LANES), jnp.int32)
    in_specs = [pl.BlockSpec((1, blk, nslot), lambda i: (i, 0, 0), memory_space=pltpu.SMEM),
                pl.BlockSpec((1, blk, nslot), lambda i: (jnp.minimum(i + 1, nb - 1), 0, 0),
                             memory_space=pltpu.SMEM),
                pl.BlockSpec((1, nall, blk), lambda i: (i, 0, 0)),
                pl.BlockSpec((blk, d), lambda i: (i, 0)),
                pl.BlockSpec((1, 1, d), per_batch),
                pl.BlockSpec((1, 1, d), per_batch),
                pl.BlockSpec((1, 1, d), per_batch),
                vec, vec,
                pl.BlockSpec(memory_space=pl.ANY)]
    args = [idx, idx, gates, x.reshape(b * l, d), shift, scale, g2, gamma, beta, uv]
    if pre_rows is not None:
        in_specs.append(pl.BlockSpec((blk * (nall - nslot), ROW_WORDS * LANES), lambda i: (i, 0)))
        args.append(pre_rows)
    out = pl.pallas_call(
        functools.partial(_peer_experts_kernel, alpha),
        grid=(nb,),
        in_specs=in_specs,
        out_specs=pl.BlockSpec((blk, d), lambda i: (i, 0)),
        out_shape=jax.ShapeDtypeStruct((b * l, d), F32),
        scratch_shapes=[gather_buf, gather_buf, pltpu.VMEM((blk, d), F32), pltpu.SemaphoreType.DMA((2,))],
        compiler_params=_cparams(("arbitrary",)),
    )(*args)
    return out.reshape(b, l, d)


SC_CORES = 2
SC_SUBCORES = 16
SC_GATHER_ROWS = 64
PEER_SC_SLOTS = 64
PEER_SC_MIN_TOKENS = 2048


def sc_gather_rows(table, idx):
    n = idx.shape[0]
    width = table.shape[1]
    workers = SC_CORES * SC_SUBCORES
    chunks = n // SC_GATHER_ROWS
    per_worker = chunks // workers
    assert n == per_worker * workers * SC_GATHER_ROWS

    def body(table_hbm, idx_hbm, out_hbm, idx_v, rows_v, sem):
        wid = lax.axis_index("s") * SC_CORES + lax.axis_index("c")

        @pl.loop(0, per_worker)
        def _(j):
            g = wid * per_worker + j
            pltpu.sync_copy(idx_hbm.at[g], idx_v)
            pltpu.async_copy(table_hbm.at[idx_v], rows_v, sem).wait()
            pltpu.sync_copy(rows_v, out_hbm.at[pl.ds(g * SC_GATHER_ROWS, SC_GATHER_ROWS)])

    return pl.kernel(
        body,
        out_type=jax.ShapeDtypeStruct((n, width), table.dtype),
        mesh=plsc.VectorSubcoreMesh(core_axis_name="c", subcore_axis_name="s"),
        scratch_types=[pltpu.VMEM((SC_GATHER_ROWS,), jnp.int32),
                       pltpu.VMEM((SC_GATHER_ROWS, width), table.dtype),
                       pltpu.SemaphoreType.DMA],
    )(table, idx.reshape(chunks, SC_GATHER_ROWS))


def peer_experts_split(alpha, x, shift, scale, g2, gamma, beta, idx_t, gates_t, uv):
    b, l, _ = x.shape
    if l < PEER_SC_MIN_TOKENS:
        return peer_experts(alpha, x, shift, scale, g2, gamma, beta, idx_t, gates_t, uv)
    ntc = idx_t.shape[1] - PEER_SC_SLOTS
    table = uv.reshape(uv.shape[0], ROW_WORDS * LANES)
    pre = [sc_gather_rows(table, idx_t[i, ntc:, :].T.reshape(-1)) for i in range(b)]
    outs = [peer_experts(alpha, x[i:i + 1], shift[i:i + 1], scale[i:i + 1], g2[i:i + 1], gamma, beta,
                         idx_t[i:i + 1, :ntc], gates_t[i:i + 1], uv, pre[i]) for i in range(b)]
    return jnp.concatenate(outs, axis=0)


def _to_col_major(t, rows):
    b, l, c = t.shape
    return t.reshape(b, rows, GRID_W, c).transpose(0, 2, 1, 3).reshape(b, l, c)


def _from_col_major(t, rows):
    b, l, c = t.shape
    return t.reshape(b, GRID_W, rows, c).transpose(0, 2, 1, 3).reshape(b, l, c)


def _pad_heads(t):
    b, l, _ = t.shape
    t = t.reshape(b, l, GLA_HEADS, GLA_DK)
    t = jnp.pad(t, ((0, 0), (0, 0), (0, 0), (0, LANES - GLA_DK)))
    return t.reshape(b, l, GLA_HEADS * LANES)


_HG_WIDTH = 5 * HG_HEADS * LANES
_GLA_WIDTH = GLA_HEADS * LANES


def _project_weights(w):
    o, kw = _HG_WIDTH, GLA_HEADS * GLA_DK
    pad_cols = lambda cols: _pad_heads(cols[None])[0]
    w = jnp.concatenate([w[:, :o], pad_cols(w[:, o:o + kw]), pad_cols(w[:, o + kw:o + 2 * kw]),
                         w[:, o + 2 * kw:]], axis=1)
    return jnp.pad(w, ((0, 0), (0, -w.shape[1] % LANES))).astype(BF16)


def _gla_parts(u, rows):
    o, gw = _HG_WIDTH, _GLA_WIDTH
    order = (lambda t: t) if rows is None else (lambda t: _to_col_major(t, rows))
    q, k, v = (order(u[..., o + n * gw:o + (n + 1) * gw]) for n in range(3))
    r0 = o + 4 * gw
    rf = order(u[..., r0:r0 + GLA_GATE_RANK])
    rb = order(u[..., r0 + GLA_GATE_RANK:r0 + 2 * GLA_GATE_RANK])
    return q, k, v, rf, rb


def _token_tile(l, want):
    while l % want:
        want //= 2
    return want


def kernel(x, c, ctx, c_ctx, w_ada, b_ada, w_in, w_gk2, b_gk, hg_lower_bounds, hg_norm, gla_norm, w_out,
           ln_gamma, ln_beta, peer_w_query, peer_sub_keys, peer_u, peer_v):
    batch, seq, d = x.shape
    depth = w_ada.shape[0]
    rows = seq // GRID_W
    alpha = (2.0 * depth) ** 0.25

    sm = jax.nn.softmax(hg_lower_bounds.astype(F32), axis=1)
    lb_all = jnp.clip(jnp.cumsum(sm, axis=1) - sm[:, :1], 0.0, 1.0 - 1e-6)
    hg_consts = jnp.stack([jnp.log(jnp.maximum(lb_all, LB_FLOOR)), jnp.log1p(-lb_all), 1.0 - lb_all], axis=2)

    cs = jnp.concatenate([c, c_ctx[None, :], jnp.zeros((SUBLANES - 1 - batch % SUBLANES, d), F32)], axis=0)
    mods = ada_mods(cs, w_ada, b_ada)

    zero_state = jnp.zeros((batch, HG_HEADS, LANES, LANES), F32)
    lat_tile = _token_tile(seq, 256)
    ctx_tile = _token_tile(ctx.shape[1], 256)
    gw_pad = lambda w: _pad_heads(w[None])[0]

    xc = ctx
    for i in range(depth):
        last = i == depth - 1
        m_l = mods[i, :batch].reshape(batch, 1, 6, d)
        m_c = jnp.broadcast_to(mods[i, batch].reshape(1, 1, 6, d), (batch, 1, 6, d))
        w_in_b = _project_weights(w_in[i])
        w_out_b = w_out[i].astype(BF16)
        w_q_b = peer_w_query[i].astype(BF16)
        gam, bet = ln_gamma[i], ln_beta[i]
        hgain, ggain = hg_norm[i][None, :], gla_norm[i][None, :]
        gk_w = [gw_pad(w_gk2[i, dd]) for dd in range(2)]
        gk_b = [gw_pad(b_gk[i, dd][None, :]) for dd in range(2)]
        uv = pack_experts(peer_u[i], peer_v[i])

        u_c = ln_mod_matmul(xc, m_c[:, :, 0], m_c[:, :, 1], w_in_b, ctx_tile)
        u_l = ln_mod_matmul(x, m_l[:, :, 0], m_l[:, :, 1], w_in_b, lat_tile)
        h_out, g_out = {}, {}
        gc = _gla_parts(u_c, None)
        gl = _gla_parts(u_l, rows)
        for dd, rev in ((0, False), (1, True)):
            o_c, s_c = hgrn_scan(u_c, 1 + dd, hg_consts[dd, i], zero_state, rev)
            o_l, _ = hgrn_scan(u_l, 1 + dd, hg_consts[dd, i], s_c, rev)
            h_out[dd] = (o_c, o_l)
            o_c, s_c = gla_scan(gc[0], gc[1], gc[2], gc[3 + dd], gk_w[dd], gk_b[dd], zero_state, rev)
            o_l, _ = gla_scan(gl[0], gl[1], gl[2], gl[3 + dd], gk_w[dd], gk_b[dd], s_c, rev)
            g_out[dd] = (o_c, _from_col_major(o_l, rows))
        x = mix_out(alpha, h_out[0][1], h_out[1][1], g_out[0][1], g_out[1][1], u_l, x, m_l[:, :, 2],
                    hgain, ggain, w_out_b, gam[0:1], bet[0:1], lat_tile)
        if not last:
            xc = mix_out(alpha, h_out[0][0], h_out[1][0], g_out[0][0], g_out[1][0], u_c, xc, m_c[:, :, 2],
                         hgain, ggain, w_out_b, gam[0:1], bet[0:1], ctx_tile)

        def peer_block(h, m, tile):
            qh = ln_mod_matmul(h, m[:, :, 3], m[:, :, 4], w_q_b, tile)
            idx_t, gates_t = peer_route(qh, peer_sub_keys[i], _token_tile(h.shape[1], LANES))
            return peer_experts_split(alpha, h, m[:, :, 3], m[:, :, 4], m[:, :, 5], gam[1:2], bet[1:2],
                                      idx_t, gates_t, uv)

        x = peer_block(x, m_l, lat_tile)
        if not last:
            xc = peer_block(xc, m_c, ctx_tile)
    return x
```

```python
import functools

import jax
import jax.numpy as jnp
from jax import lax
from jax.experimental import pallas as pl
from jax.experimental.pallas import tpu as pltpu

F32 = jnp.float32
BF16 = jnp.bfloat16

GRID_W = 64
HG_HEADS = 4
HG_DK = 128
GLA_HEADS = 4
GLA_DK = 64
GLA_DV = 128
GLA_GATE_RANK = 16
GLA_GATE_NORM = 16.0
PEER_HEADS = 8
PEER_NKEYS = 128
PEER_TOPK = 16
EPS = 1e-6
LB_FLOOR = 1e-30

LANES = 128
SUBLANES = 8
VMEM_LIMIT = 56 * 1024 * 1024

_NT = (((1,), (1,)), ((), ()))
_TN = (((0,), (0,)), ((), ()))


def _cparams(sem):
    return pltpu.CompilerParams(dimension_semantics=sem, vmem_limit_bytes=VMEM_LIMIT)


def _layer_norm(x):
    mu = jnp.mean(x, axis=-1, keepdims=True)
    xc = x - mu
    var = jnp.mean(xc * xc, axis=-1, keepdims=True)
    return xc * lax.rsqrt(var + EPS)


def _silu(x):
    return x * jax.nn.sigmoid(x)


def _ada_kernel(c_ref, w_ref, b_ref, o_ref):
    s = _silu(c_ref[...]).astype(BF16)
    o_ref[0] = jnp.dot(s, w_ref[0].astype(BF16), preferred_element_type=F32) + b_ref[0]


def ada_mods(cs, w_ada, b_ada):
    depth, d, n = w_ada.shape
    r = cs.shape[0]
    tn = n // 4
    return pl.pallas_call(
        _ada_kernel,
        grid=(depth, n // tn),
        in_specs=[pl.BlockSpec((r, d), lambda l, j: (0, 0)),
                  pl.BlockSpec((1, d, tn), lambda l, j: (l, 0, j)),
                  pl.BlockSpec((1, 1, tn), lambda l, j: (l, 0, j))],
        out_specs=pl.BlockSpec((1, r, tn), lambda l, j: (l, 0, j)),
        out_shape=jax.ShapeDtypeStruct((depth, r, n), F32),
        compiler_params=_cparams(("parallel", "parallel")),
    )(cs, w_ada, b_ada.reshape(depth, 1, n))


def _ln_mod_matmul_kernel(x_ref, sh_ref, sc_ref, w_ref, o_ref):
    y = _layer_norm(x_ref[0]) * (1.0 + sc_ref[0]) + sh_ref[0]
    o_ref[0] = jnp.dot(y.astype(BF16), w_ref[...], preferred_element_type=F32)


def ln_mod_matmul(x, shift, scale, w_bf16, tile):
    b, l, d = x.shape
    n = w_bf16.shape[1]
    return pl.pallas_call(
        _ln_mod_matmul_kernel,
        grid=(b, l // tile),
        in_specs=[pl.BlockSpec((1, tile, d), lambda i, j: (i, j, 0)),
                  pl.BlockSpec((1, 1, d), lambda i, j: (i, 0, 0)),
                  pl.BlockSpec((1, 1, d), lambda i, j: (i, 0, 0)),
                  pl.BlockSpec((d, n), lambda i, j: (0, 0))],
        out_specs=pl.BlockSpec((1, tile, n), lambda i, j: (i, j, 0)),
        out_shape=jax.ShapeDtypeStruct((b, l, n), F32),
        compiler_params=_cparams(("parallel", "parallel")),
    )(x, shift, scale, w_bf16)


def _chunk_scan(q, k, v, la, st_ref, reverse):
    c, dk = q.shape
    nlev = c.bit_length() - 1
    row = lax.broadcasted_iota(jnp.int32, (c, dk), 0)
    qi = lax.broadcasted_iota(jnp.int32, (c, c), 0)
    ki = lax.broadcasted_iota(jnp.int32, (c, c), 1)
    late, early = (ki, qi) if reverse else (qi, ki)
    scores = jnp.where(qi == ki,
                       lax.dot_general(q.astype(BF16), k.astype(BF16), _NT, preferred_element_type=F32), 0.0)
    p = la
    t = la
    for lev in range(nlev):
        m = 1 << lev
        right = (row & m) != 0
        if reverse:
            d = jnp.where(right, p - la, t - p + la)
        else:
            d = jnp.where(right, p, t - p)
        e = jnp.exp(d)
        s_l = lax.dot_general((q * e).astype(BF16), (k * e).astype(BF16), _NT, preferred_element_type=F32)
        lb = late >> lev
        mask = (lb == (early >> lev) + 1) & ((lb & 1) == 1)
        scores = jnp.where(mask, s_l, scores)
        sib = jnp.where(right, pltpu.roll(t, m, 0), pltpu.roll(t, c - m, 0))
        p = p + jnp.where(right, sib, 0.0)
        t = t + sib
    if reverse:
        dq, dkk = t - p + la, p - la
    else:
        dq, dkk = p, t - p
    st = st_ref[...]
    qs = (q * jnp.exp(dq)).astype(BF16)
    ks = (k * jnp.exp(dkk)).astype(BF16)
    vb = v.astype(BF16)
    o = (jnp.dot(scores.astype(BF16), vb, preferred_element_type=F32)
         + lax.dot_general(qs, st.astype(BF16), _NT, preferred_element_type=F32))
    st_ref[...] = st * jnp.exp(t[0:1, :]) + lax.dot_general(vb, ks, _TN, preferred_element_type=F32)
    return o


def _softplus_neg_abs(d):
    return jnp.log(1.0 + jnp.exp(-jnp.abs(d)))


def _log_sigmoid(z):
    return jnp.minimum(z, 0.0) - _softplus_neg_abs(z)


def _hgrn_gate(z, c0, c1, om):
    ls = _log_sigmoid(z)
    y = c1 + ls
    la = jnp.maximum(c0, y) + _softplus_neg_abs(c0 - y)
    return om * jax.nn.sigmoid(-z), la


def _scan_prologue(s0_ref, st_ref):
    @pl.when(pl.program_id(1) == 0)
    def _():
        st_ref[...] = s0_ref[0]


def _scan_epilogue(sfin_ref, st_ref):
    @pl.when(pl.program_id(1) == pl.num_programs(1) - 1)
    def _():
        sfin_ref[0] = st_ref[...]


SCAN_CHUNK = 128
SCAN_STEP_CHUNKS = 2


def _step_chunks(ref, reverse):
    n = ref.shape[1] // _scan_chunk(ref.shape[1])
    c = ref.shape[1] // n
    order = range(n - 1, -1, -1) if reverse else range(n)
    return [slice(i * c, (i + 1) * c) for i in order]


def _hgrn_scan_kernel(reverse, q_ref, v_ref, z_ref, cst_ref, s0_ref, o_ref, sfin_ref, st_ref):
    _scan_prologue(s0_ref, st_ref)
    for rows in _step_chunks(q_ref, reverse):
        for h in range(HG_HEADS):
            cols = slice(h * LANES, (h + 1) * LANES)
            k, la = _hgrn_gate(z_ref[0, rows, cols], cst_ref[0:1, cols], cst_ref[1:2, cols], cst_ref[2:3, cols])
            o_ref[0, rows, cols] = _chunk_scan(q_ref[0, rows, cols], k, v_ref[0, rows, cols], la,
                                               st_ref.at[h], reverse)
    _scan_epilogue(sfin_ref, st_ref)


def _gla_scan_kernel(reverse, q_ref, k_ref, v_ref, r_ref, w_ref, b_ref, s0_ref, o_ref, sfin_ref, st_ref):
    _scan_prologue(s0_ref, st_ref)
    g = jnp.dot(r_ref[0].astype(BF16), w_ref[...].astype(BF16), preferred_element_type=F32) + b_ref[...]
    la_all = _log_sigmoid(g) * (1.0 / GLA_GATE_NORM)
    for rows in _step_chunks(q_ref, reverse):
        for h in range(GLA_HEADS):
            cols = slice(h * LANES, (h + 1) * LANES)
            q = q_ref[0, rows, cols] * (GLA_DK ** -0.5)
            o_ref[0, rows, cols] = _chunk_scan(q, k_ref[0, rows, cols], v_ref[0, rows, cols], la_all[rows, cols],
                                               st_ref.at[h], reverse)
    _scan_epilogue(sfin_ref, st_ref)


def _scan_chunk(l):
    return SCAN_CHUNK if l % SCAN_CHUNK == 0 else SCAN_CHUNK // 2


def _scan_block(l):
    c = _scan_chunk(l)
    return c * SCAN_STEP_CHUNKS if l % (c * SCAN_STEP_CHUNKS) == 0 else c


def _scan_call(kernel, reverse, batch, l, heads, in_arrays, in_specs, s0):
    c = _scan_block(l)
    n = l // c
    width = heads * LANES
    state_spec = pl.BlockSpec((1, heads, LANES, LANES), lambda b, j: (b, 0, 0, 0))
    return pl.pallas_call(
        functools.partial(kernel, reverse),
        grid=(batch, n),
        in_specs=in_specs + [state_spec],
        out_specs=[pl.BlockSpec((1, c, width), lambda b, j: (b, (n - 1 - j) if reverse else j, 0)),
                   state_spec],
        out_shape=[jax.ShapeDtypeStruct((batch, l, width), F32),
                   jax.ShapeDtypeStruct((batch, heads, LANES, LANES), F32)],
        scratch_shapes=[pltpu.VMEM((heads, LANES, LANES), F32)],
        compiler_params=_cparams(("parallel", "arbitrary")),
    )(*in_arrays, s0)


def hgrn_scan(u, gate_block, consts, s0, reverse):
    b, l, _ = u.shape
    c = _scan_block(l)
    n = l // c
    width = HG_HEADS * LANES

    def col(block):
        return pl.BlockSpec((1, c, width), lambda i, j: (i, (n - 1 - j) if reverse else j, block))

    in_specs = [col(0), col(3), col(gate_block), pl.BlockSpec((3, width), lambda i, j: (0, 0))]
    return _scan_call(_hgrn_scan_kernel, reverse, b, l, HG_HEADS, (u, u, u, consts), in_specs, s0)


def gla_scan(q, k, v, r, w, bias, s0, reverse):
    b, l, width = q.shape
    c = _scan_block(l)
    n = l // c
    rank = r.shape[-1]

    def col(wd):
        return pl.BlockSpec((1, c, wd), lambda i, j: (i, (n - 1 - j) if reverse else j, 0))

    in_specs = [col(width), col(width), col(width), col(rank),
                pl.BlockSpec((rank, width), lambda i, j: (0, 0)),
                pl.BlockSpec((1, width), lambda i, j: (0, 0))]
    return _scan_call(_gla_scan_kernel, reverse, b, l, GLA_HEADS, (q, k, v, r, w, bias), in_specs, s0)


def _head_rms(a, gain, heads):
    outs = []
    for h in range(heads):
        ah = a[:, h * LANES:(h + 1) * LANES]
        outs.append(ah * lax.rsqrt(jnp.mean(ah * ah, axis=-1, keepdims=True) + EPS) * gain)
    return jnp.concatenate(outs, axis=1)


def _mix_out_kernel(alpha, hf_ref, hb_ref, gf_ref, gb_ref, hgate_ref, ggate_ref, x_ref, g1_ref,
                    hgain_ref, ggain_ref, w_ref, gam_ref, bet_ref, o_ref):
    ma = _head_rms(hf_ref[0] + hb_ref[0], hgain_ref[...], HG_HEADS) * _silu(hgate_ref[0])
    mb = _head_rms(gf_ref[0] + gb_ref[0], ggain_ref[...], GLA_HEADS) * _silu(ggate_ref[0])
    mix = jnp.concatenate([ma, mb], axis=1).astype(BF16)
    y = jnp.dot(mix, w_ref[...], preferred_element_type=F32)
    r = alpha * x_ref[0] + g1_ref[0] * y
    o_ref[0] = _layer_norm(r) * gam_ref[...] + bet_ref[...]


def mix_out(alpha, hf, hb, gf, gb, u, x, g1, hgain, ggain, w_out_bf16, gamma, beta, tile):
    b, l, d = x.shape
    hw = HG_HEADS * LANES
    tok = lambda width, blk: pl.BlockSpec((1, tile, width), lambda i, j: (i, j, blk))
    vec = lambda width: pl.BlockSpec((1, width), lambda i, j: (0, 0))
    return pl.pallas_call(
        functools.partial(_mix_out_kernel, alpha),
        grid=(b, l // tile),
        in_specs=[tok(hw, 0), tok(hw, 0), tok(hw, 0), tok(hw, 0),
                  tok(hw, 4), tok(hw, 8), tok(d, 0),
                  pl.BlockSpec((1, 1, d), lambda i, j: (i, 0, 0)),
                  vec(LANES), vec(LANES),
                  pl.BlockSpec(w_out_bf16.shape, lambda i, j: (0, 0)),
                  vec(d), vec(d)],
        out_specs=tok(d, 0),
        out_shape=jax.ShapeDtypeStruct((b, l, d), F32),
        compiler_params=_cparams(("parallel", "parallel")),
    )(hf, hb, gf, gb, u, u, x, g1, hgain, ggain, w_out_bf16, gamma, beta)


_CAND_PIECES = ((0, 16),) + tuple((a, 8) for a in range(1, 8))
_CAND_TAIL = 8
_NCAND = sum(n for _, n in _CAND_PIECES) + _CAND_TAIL


def _split_bf16(x):
    hi = x.astype(BF16)
    return hi, (x - hi.astype(F32)).astype(BF16)


def _dot3_nt(a, b):
    (ah, al), (bh, bl) = a, b
    dot = lambda x, y: lax.dot_general(x, y, _NT, preferred_element_type=F32)
    return dot(ah, bh) + (dot(ah, bl) + dot(al, bh))


def _extract_top(s, iota, nrows):
    m = jnp.max(s, axis=0, keepdims=True)
    pos = jnp.min(jnp.where(s == m, iota, nrows), axis=0, keepdims=True)
    return m, pos, iota == pos


def _peer_route_kernel(q_ref, k1_ref, k2_ref, idx_ref, g_ref, sc, v1, i1, v2, i2, cs, ci, tv):
    tt = q_ref.shape[1]
    nk, topk = PEER_NKEYS, PEER_TOPK
    keys = (_split_bf16(k1_ref[...]), _split_bf16(k2_ref[...]))
    for h in range(PEER_HEADS):
        for side in range(2):
            o = nk * (2 * h + side)
            sc[2 * h + side] = _dot3_nt(keys[side], _split_bf16(q_ref[0, :, o:o + nk]))
    iota = lax.broadcasted_iota(jnp.int32, (nk, tt), 0)
    iota2 = lax.broadcasted_iota(jnp.int32, (_NCAND, tt), 0)

    for h in range(PEER_HEADS + 1):
        first, second = h < PEER_HEADS, h >= 1
        cur, prev = h % 2, (h - 1) % 2
        out0 = (h - 1) * topk
        init = ()
        if first:
            init += (sc[2 * h], sc[2 * h + 1])
        if second:
            row = 0
            for a, n in _CAND_PIECES:
                cs[row:row + n, :] = v1[prev, a:a + 1, :] + v2[prev, 0:n, :]
                ci[row:row + n, :] = i1[prev, a:a + 1, :] * nk + i2[prev, 0:n, :]
                row += n
            cs[row:row + _CAND_TAIL, :] = v1[prev, topk - _CAND_TAIL:topk, :] + v2[prev, 0:1, :]
            ci[row:row + _CAND_TAIL, :] = i1[prev, topk - _CAND_TAIL:topk, :] * nk + i2[prev, 0:1, :]
            init += (cs[...],)

        def step(j, st, first=first, second=second, cur=cur, out0=out0):
            out = ()
            if first:
                s1, s2 = st[0], st[1]
                m1, p1, hit1 = _extract_top(s1, iota, nk)
                m2, p2, hit2 = _extract_top(s2, iota, nk)
                v1[cur, pl.ds(j, 1), :] = m1
                i1[cur, pl.ds(j, 1), :] = p1
                v2[cur, pl.ds(j, 1), :] = m2
                i2[cur, pl.ds(j, 1), :] = p2
                out += (jnp.where(hit1, -jnp.inf, s1), jnp.where(hit2, -jnp.inf, s2))
            if second:
                s = st[-1]
                m, _, hit = _extract_top(s, iota2, _NCAND)
                tv[pl.ds(j, 1), :] = m
                idx_ref[0, pl.ds(out0 + j, 1), :] = jnp.max(jnp.where(hit, ci[...], -1), axis=0, keepdims=True)
                out += (jnp.where(hit, -jnp.inf, s),)
            return out

        lax.fori_loop(0, topk, step, init)
        if second:
            top = tv[...]
            e = jnp.exp(top - top[0:1, :])
            g_ref[0, out0:out0 + topk, :] = e / jnp.sum(e, axis=0, keepdims=True)


def peer_route(q, sub_keys, tile):
    b, l, dq = q.shape
    nslot = PEER_HEADS * PEER_TOPK
    out_spec = pl.BlockSpec((1, nslot, tile), lambda i, j: (i, 0, j))
    keys = pl.BlockSpec((PEER_NKEYS, PEER_NKEYS), lambda i, j: (0, 0))
    return pl.pallas_call(
        _peer_route_kernel,
        grid=(b, l // tile),
        in_specs=[pl.BlockSpec((1, tile, dq), lambda i, j: (i, j, 0)), keys, keys],
        out_specs=[out_spec, out_spec],
        out_shape=[jax.ShapeDtypeStruct((b, nslot, l), jnp.int32),
                   jax.ShapeDtypeStruct((b, nslot, l), F32)],
        scratch_shapes=[pltpu.VMEM((2 * PEER_HEADS, PEER_NKEYS, tile), F32),
                        pltpu.VMEM((2, PEER_TOPK, tile), F32), pltpu.VMEM((2, PEER_TOPK, tile), jnp.int32),
                        pltpu.VMEM((2, PEER_TOPK, tile), F32), pltpu.VMEM((2, PEER_TOPK, tile), jnp.int32),
                        pltpu.VMEM((_NCAND, tile), F32), pltpu.VMEM((_NCAND, tile), jnp.int32),
                        pltpu.VMEM((PEER_TOPK, tile), F32)],
        compiler_params=_cparams(("parallel", "parallel")),
    )(q, sub_keys[0], sub_keys[1])


PEER_TOK_TILE = 8
PEER_ISSUE_SLACK = 2
ROW_WORDS = SUBLANES
PAIR_ROWS = ROW_WORDS // 2


def _unpack_pairs(words):
    lo = pltpu.bitcast(words << 16, F32)
    hi = pltpu.bitcast(words & jnp.int32(-65536), F32)
    return lo, hi


def _expert_token(words, hm, g_ref, y_ref, tok):
    hrow = lambda c: hm[tok:tok + 1, c * LANES:(c + 1) * LANES]
    acc = None
    for j in range(PAIR_ROWS):
        lo, hi = _unpack_pairs(words(j))
        part = lo * hrow(2 * j) + hi * hrow(2 * j + 1)
        acc = part if acc is None else acc + part
    a = jnp.sum(acc, axis=1, keepdims=True)
    w = jax.nn.gelu(a) * g_ref[0, :, tok:tok + 1]
    for j in range(PAIR_ROWS):
        lo, hi = _unpack_pairs(words(PAIR_ROWS + j))
        for c, vc in ((2 * j, lo), (2 * j + 1, hi)):
            y_ref[tok:tok + 1, c * LANES:(c + 1) * LANES] = jnp.sum(vc * w, axis=0, keepdims=True)


def _peer_experts_kernel(alpha, ids_ref, ids_next_ref, g_ref, x_ref, sh_ref, sc_ref, g2_ref, gam_ref, bet_ref,
                         uv_hbm, o_ref, buf_a, buf_b, y_ref, sem):
    tt = PEER_TOK_TILE
    d = x_ref.shape[1]
    nslot = g_ref.shape[1]
    step = pl.program_id(0)
    last = pl.num_programs(0) - 1
    bufs = (buf_a, buf_b)

    def row_copy(expert, dst, row, s):
        return pltpu.make_async_copy(uv_hbm.at[expert], dst.at[pl.ds(row, ROW_WORDS)], sem.at[s])

    def tile_wait(buf, s):
        pltpu.make_async_copy(buf, buf, sem.at[s]).wait()

    @pl.when(step == 0)
    def _():
        def per_token(t, carry):
            def per_group(g, carry):
                for j in range(SUBLANES):
                    k = g * SUBLANES + j
                    row = pl.multiple_of((t * nslot + k) * ROW_WORDS, ROW_WORDS)
                    row_copy(ids_ref[0, t, k], buf_a, row, 0).start(priority=j % 2)
                return carry
            return lax.fori_loop(0, nslot // SUBLANES, per_group, carry)
        lax.fori_loop(0, tt, per_token, 0)

    x = x_ref[...]
    hm = _layer_norm(x) * (1.0 + sc_ref[0]) + sh_ref[0]

    issue_tokens = tt - PEER_ISSUE_SLACK
    per_token_issue = -(-tt * nslot // issue_tokens)

    for phase in range(2):
        cur, nxt = bufs[phase], bufs[1 - phase]
        tile_wait(cur, phase)
        for t in range(tt):
            for n in range(t * per_token_issue, min((t + 1) * per_token_issue, tt * nslot)):
                tn, k = divmod(n, nslot)
                expert = ids_ref[0, tt + tn, k] if phase == 0 else ids_next_ref[0, tn, k]
                row_copy(expert, nxt, n * ROW_WORDS, 1 - phase).start(priority=n % 2)
            tok = phase * tt + t
            base = t * nslot * ROW_WORDS
            words = lambda j, base=base, cur=cur: cur[pl.ds(base + j, nslot, stride=ROW_WORDS), :]
            _expert_token(words, hm, g_ref, y_ref, tok)

    r = alpha * x + g2_ref[0] * y_ref[...]
    o_ref[...] = _layer_norm(r) * gam_ref[...] + bet_ref[...]

    @pl.when(step == last)
    def _():
        tile_wait(buf_a, 0)


def pack_experts(u, v):
    def pack(a):
        bits = lax.bitcast_convert_type(a.astype(BF16).reshape(a.shape[0], PAIR_ROWS, 2, LANES), jnp.uint16)
        bits = bits.astype(jnp.uint32)
        return bits[:, :, 0, :] | (bits[:, :, 1, :] << 16)
    return lax.bitcast_convert_type(jnp.concatenate([pack(u), pack(v)], axis=1), jnp.int32)


def peer_experts(alpha, x, shift, scale, g2, gamma, beta, idx_t, gates_t, uv):
    b, l, d = x.shape
    assert d == 2 * PAIR_ROWS * LANES
    nslot = idx_t.shape[1]
    blk = 2 * PEER_TOK_TILE
    blocks_per_batch = l // blk
    nb = b * blocks_per_batch
    idx = idx_t.transpose(0, 2, 1).reshape(nb, blk, nslot)
    gates = gates_t.reshape(b, nslot, blocks_per_batch, blk).transpose(0, 2, 1, 3).reshape(nb, nslot, blk)
    per_batch = lambda i: (i // blocks_per_batch, 0, 0)
    vec = pl.BlockSpec((1, d), lambda i: (0, 0))
    gather_buf = pltpu.VMEM((PEER_TOK_TILE * nslot * ROW_WORDS, LANES), jnp.int32)
    out = pl.pallas_call(
        functools.partial(_peer_experts_kernel, alpha),
        grid=(nb,),
        in_specs=[pl.BlockSpec((1, blk, nslot), lambda i: (i, 0, 0), memory_space=pltpu.SMEM),
                  pl.BlockSpec((1, blk, nslot), lambda i: (jnp.minimum(i + 1, nb - 1), 0, 0),
                               memory_space=pltpu.SMEM),
                  pl.BlockSpec((1, nslot, blk), lambda i: (i, 0, 0)),
                  pl.BlockSpec((blk, d), lambda i: (i, 0)),
                  pl.BlockSpec((1, 1, d), per_batch),
                  pl.BlockSpec((1, 1, d), per_batch),
                  pl.BlockSpec((1, 1, d), per_batch),
                  vec, vec,
                  pl.BlockSpec(memory_space=pl.ANY)],
        out_specs=pl.BlockSpec((blk, d), lambda i: (i, 0)),
        out_shape=jax.ShapeDtypeStruct((b * l, d), F32),
        scratch_shapes=[gather_buf, gather_buf, pltpu.VMEM((blk, d), F32), pltpu.SemaphoreType.DMA((2,))],
        compiler_params=_cparams(("arbitrary",)),
    )(idx, idx, gates, x.reshape(b * l, d), shift, scale, g2, gamma, beta, uv)
    return out.reshape(b, l, d)


def _to_col_major(t, rows):
    b, l, c = t.shape
    return t.reshape(b, rows, GRID_W, c).transpose(0, 2, 1, 3).reshape(b, l, c)


def _from_col_major(t, rows):
    b, l, c = t.shape
    return t.reshape(b, GRID_W, rows, c).transpose(0, 2, 1, 3).reshape(b, l, c)


def _pad_heads(t):
    b, l, _ = t.shape
    t = t.reshape(b, l, GLA_HEADS, GLA_DK)
    t = jnp.pad(t, ((0, 0), (0, 0), (0, 0), (0, LANES - GLA_DK)))
    return t.reshape(b, l, GLA_HEADS * LANES)


_HG_WIDTH = 5 * HG_HEADS * LANES
_GLA_WIDTH = GLA_HEADS * LANES


def _project_weights(w):
    o, kw = _HG_WIDTH, GLA_HEADS * GLA_DK
    pad_cols = lambda cols: _pad_heads(cols[None])[0]
    w = jnp.concatenate([w[:, :o], pad_cols(w[:, o:o + kw]), pad_cols(w[:, o + kw:o + 2 * kw]),
                         w[:, o + 2 * kw:]], axis=1)
    return jnp.pad(w, ((0, 0), (0, -w.shape[1] % LANES))).astype(BF16)


def _gla_parts(u, rows):
    o, gw = _HG_WIDTH, _GLA_WIDTH
    order = (lambda t: t) if rows is None else (lambda t: _to_col_major(t, rows))
    q, k, v = (order(u[..., o + n * gw:o + (n + 1) * gw]) for n in range(3))
    r0 = o + 4 * gw
    rf = order(u[..., r0:r0 + GLA_GATE_RANK])
    rb = order(u[..., r0 + GLA_GATE_RANK:r0 + 2 * GLA_GATE_RANK])
    return q, k, v, rf, rb


def _token_tile(l, want):
    while l % want:
        want //= 2
    return want


def kernel(x, c, ctx, c_ctx, w_ada, b_ada, w_in, w_gk2, b_gk, hg_lower_bounds, hg_norm, gla_norm, w_out,
           ln_gamma, ln_beta, peer_w_query, peer_sub_keys, peer_u, peer_v):
    batch, seq, d = x.shape
    depth = w_ada.shape[0]
    rows = seq // GRID_W
    alpha = (2.0 * depth) ** 0.25

    sm = jax.nn.softmax(hg_lower_bounds.astype(F32), axis=1)
    lb_all = jnp.clip(jnp.cumsum(sm, axis=1) - sm[:, :1], 0.0, 1.0 - 1e-6)
    hg_consts = jnp.stack([jnp.log(jnp.maximum(lb_all, LB_FLOOR)), jnp.log1p(-lb_all), 1.0 - lb_all], axis=2)

    cs = jnp.concatenate([c, c_ctx[None, :], jnp.zeros((SUBLANES - 1 - batch % SUBLANES, d), F32)], axis=0)
    mods = ada_mods(cs, w_ada, b_ada)

    zero_state = jnp.zeros((batch, HG_HEADS, LANES, LANES), F32)
    lat_tile = _token_tile(seq, 256)
    ctx_tile = _token_tile(ctx.shape[1], 256)
    gw_pad = lambda w: _pad_heads(w[None])[0]

    xc = ctx
    for i in range(depth):
        last = i == depth - 1
        m_l = mods[i, :batch].reshape(batch, 1, 6, d)
        m_c = jnp.broadcast_to(mods[i, batch].reshape(1, 1, 6, d), (batch, 1, 6, d))
        w_in_b = _project_weights(w_in[i])
        w_out_b = w_out[i].astype(BF16)
        w_q_b = peer_w_query[i].astype(BF16)
        gam, bet = ln_gamma[i], ln_beta[i]
        hgain, ggain = hg_norm[i][None, :], gla_norm[i][None, :]
        gk_w = [gw_pad(w_gk2[i, dd]) for dd in range(2)]
        gk_b = [gw_pad(b_gk[i, dd][None, :]) for dd in range(2)]
        uv = pack_experts(peer_u[i], peer_v[i])

        u_c = ln_mod_matmul(xc, m_c[:, :, 0], m_c[:, :, 1], w_in_b, ctx_tile)
        u_l = ln_mod_matmul(x, m_l[:, :, 0], m_l[:, :, 1], w_in_b, lat_tile)
        h_out, g_out = {}, {}
        gc = _gla_parts(u_c, None)
        gl = _gla_parts(u_l, rows)
        for dd, rev in ((0, False), (1, True)):
            o_c, s_c = hgrn_scan(u_c, 1 + dd, hg_consts[dd, i], zero_state, rev)
            o_l, _ = hgrn_scan(u_l, 1 + dd, hg_consts[dd, i], s_c, rev)
            h_out[dd] = (o_c, o_l)
            o_c, s_c = gla_scan(gc[0], gc[1], gc[2], gc[3 + dd], gk_w[dd], gk_b[dd], zero_state, rev)
            o_l, _ = gla_scan(gl[0], gl[1], gl[2], gl[3 + dd], gk_w[dd], gk_b[dd], s_c, rev)
            g_out[dd] = (o_c, _from_col_major(o_l, rows))
        x = mix_out(alpha, h_out[0][1], h_out[1][1], g_out[0][1], g_out[1][1], u_l, x, m_l[:, :, 2],
                    hgain, ggain, w_out_b, gam[0:1], bet[0:1], lat_tile)
        if not last:
            xc = mix_out(alpha, h_out[0][0], h_out[1][0], g_out[0][0], g_out[1][0], u_c, xc, m_c[:, :, 2],
                         hgain, ggain, w_out_b, gam[0:1], bet[0:1], ctx_tile)

        def peer_block(h, m, tile):
            qh = ln_mod_matmul(h, m[:, :, 3], m[:, :, 4], w_q_b, tile)
            idx_t, gates_t = peer_route(qh, peer_sub_keys[i], _token_tile(h.shape[1], LANES))
            return peer_experts(alpha, h, m[:, :, 3], m[:, :, 4], m[:, :, 5], gam[1:2], bet[1:2], idx_t, gates_t, uv)

        x = peer_block(x, m_l, lat_tile)
        if not last:
            xc = peer_block(xc, m_c, ctx_tile)
    return x
```

```python
import functools

import jax
import jax.numpy as jnp
from jax import lax
from jax.experimental import pallas as pl
from jax.experimental.pallas import tpu as pltpu

F32 = jnp.float32
BF16 = jnp.bfloat16

GRID_W = 64
HG_HEADS = 4
HG_DK = 128
GLA_HEADS = 4
GLA_DK = 64
GLA_DV = 128
GLA_GATE_RANK = 16
GLA_GATE_NORM = 16.0
PEER_HEADS = 8
PEER_NKEYS = 128
PEER_TOPK = 16
EPS = 1e-6
LB_FLOOR = 1e-30

LANES = 128
SUBLANES = 8
VMEM_LIMIT = 56 * 1024 * 1024

_NT = (((1,), (1,)), ((), ()))
_TN = (((0,), (0,)), ((), ()))


def _cparams(sem):
    return pltpu.CompilerParams(dimension_semantics=sem, vmem_limit_bytes=VMEM_LIMIT)


def _layer_norm(x):
    mu = jnp.mean(x, axis=-1, keepdims=True)
    xc = x - mu
    var = jnp.mean(xc * xc, axis=-1, keepdims=True)
    return xc * lax.rsqrt(var + EPS)


def _silu(x):
    return x * jax.nn.sigmoid(x)


def _ada_kernel(c_ref, w_ref, b_ref, o_ref):
    s = _silu(c_ref[...]).astype(BF16)
    o_ref[0] = jnp.dot(s, w_ref[0].astype(BF16), preferred_element_type=F32) + b_ref[0]


def ada_mods(cs, w_ada, b_ada):
    depth, d, n = w_ada.shape
    r = cs.shape[0]
    tn = n // 4
    return pl.pallas_call(
        _ada_kernel,
        grid=(depth, n // tn),
        in_specs=[pl.BlockSpec((r, d), lambda l, j: (0, 0)),
                  pl.BlockSpec((1, d, tn), lambda l, j: (l, 0, j)),
                  pl.BlockSpec((1, 1, tn), lambda l, j: (l, 0, j))],
        out_specs=pl.BlockSpec((1, r, tn), lambda l, j: (l, 0, j)),
        out_shape=jax.ShapeDtypeStruct((depth, r, n), F32),
        compiler_params=_cparams(("parallel", "parallel")),
    )(cs, w_ada, b_ada.reshape(depth, 1, n))


def _ln_mod_matmul_kernel(x_ref, sh_ref, sc_ref, w_ref, o_ref):
    y = _layer_norm(x_ref[0]) * (1.0 + sc_ref[0]) + sh_ref[0]
    o_ref[0] = jnp.dot(y.astype(BF16), w_ref[...], preferred_element_type=F32)


def ln_mod_matmul(x, shift, scale, w_bf16, tile):
    b, l, d = x.shape
    n = w_bf16.shape[1]
    return pl.pallas_call(
        _ln_mod_matmul_kernel,
        grid=(b, l // tile),
        in_specs=[pl.BlockSpec((1, tile, d), lambda i, j: (i, j, 0)),
                  pl.BlockSpec((1, 1, d), lambda i, j: (i, 0, 0)),
                  pl.BlockSpec((1, 1, d), lambda i, j: (i, 0, 0)),
                  pl.BlockSpec((d, n), lambda i, j: (0, 0))],
        out_specs=pl.BlockSpec((1, tile, n), lambda i, j: (i, j, 0)),
        out_shape=jax.ShapeDtypeStruct((b, l, n), F32),
        compiler_params=_cparams(("parallel", "parallel")),
    )(x, shift, scale, w_bf16)


def _chunk_scan(q, k, v, la, st_ref, reverse):
    c, dk = q.shape
    nlev = c.bit_length() - 1
    row = lax.broadcasted_iota(jnp.int32, (c, dk), 0)
    qi = lax.broadcasted_iota(jnp.int32, (c, c), 0)
    ki = lax.broadcasted_iota(jnp.int32, (c, c), 1)
    late, early = (ki, qi) if reverse else (qi, ki)
    scores = jnp.where(qi == ki,
                       lax.dot_general(q.astype(BF16), k.astype(BF16), _NT, preferred_element_type=F32), 0.0)
    p = la
    t = la
    for lev in range(nlev):
        m = 1 << lev
        right = (row & m) != 0
        if reverse:
            d = jnp.where(right, p - la, t - p + la)
        else:
            d = jnp.where(right, p, t - p)
        e = jnp.exp(d)
        s_l = lax.dot_general((q * e).astype(BF16), (k * e).astype(BF16), _NT, preferred_element_type=F32)
        lb = late >> lev
        mask = (lb == (early >> lev) + 1) & ((lb & 1) == 1)
        scores = jnp.where(mask, s_l, scores)
        sib = jnp.where(right, pltpu.roll(t, m, 0), pltpu.roll(t, c - m, 0))
        p = p + jnp.where(right, sib, 0.0)
        t = t + sib
    if reverse:
        dq, dkk = t - p + la, p - la
    else:
        dq, dkk = p, t - p
    st = st_ref[...]
    qs = (q * jnp.exp(dq)).astype(BF16)
    ks = (k * jnp.exp(dkk)).astype(BF16)
    vb = v.astype(BF16)
    o = (jnp.dot(scores.astype(BF16), vb, preferred_element_type=F32)
         + lax.dot_general(qs, st.astype(BF16), _NT, preferred_element_type=F32))
    st_ref[...] = st * jnp.exp(t[0:1, :]) + lax.dot_general(vb, ks, _TN, preferred_element_type=F32)
    return o


def _softplus_neg_abs(d):
    return jnp.log(1.0 + jnp.exp(-jnp.abs(d)))


def _log_sigmoid(z):
    return jnp.minimum(z, 0.0) - _softplus_neg_abs(z)


def _hgrn_gate(z, c0, c1, om):
    ls = _log_sigmoid(z)
    y = c1 + ls
    la = jnp.maximum(c0, y) + _softplus_neg_abs(c0 - y)
    return om * jax.nn.sigmoid(-z), la


def _scan_prologue(s0_ref, st_ref):
    @pl.when(pl.program_id(1) == 0)
    def _():
        st_ref[...] = s0_ref[0]


def _scan_epilogue(sfin_ref, st_ref):
    @pl.when(pl.program_id(1) == pl.num_programs(1) - 1)
    def _():
        sfin_ref[0] = st_ref[...]


SCAN_CHUNK = 128
SCAN_STEP_CHUNKS = 4


def _step_chunks(ref, reverse):
    n = ref.shape[1] // _scan_chunk(ref.shape[1])
    c = ref.shape[1] // n
    order = range(n - 1, -1, -1) if reverse else range(n)
    return [slice(i * c, (i + 1) * c) for i in order]


def _hgrn_scan_kernel(reverse, q_ref, v_ref, z_ref, cst_ref, s0_ref, o_ref, sfin_ref, st_ref):
    _scan_prologue(s0_ref, st_ref)
    for rows in _step_chunks(q_ref, reverse):
        for h in range(HG_HEADS):
            cols = slice(h * LANES, (h + 1) * LANES)
            k, la = _hgrn_gate(z_ref[0, rows, cols], cst_ref[0:1, cols], cst_ref[1:2, cols], cst_ref[2:3, cols])
            o_ref[0, rows, cols] = _chunk_scan(q_ref[0, rows, cols], k, v_ref[0, rows, cols], la,
                                               st_ref.at[h], reverse)
    _scan_epilogue(sfin_ref, st_ref)


def _gla_scan_kernel(reverse, q_ref, k_ref, v_ref, r_ref, w_ref, b_ref, s0_ref, o_ref, sfin_ref, st_ref):
    _scan_prologue(s0_ref, st_ref)
    g = jnp.dot(r_ref[0].astype(BF16), w_ref[...].astype(BF16), preferred_element_type=F32) + b_ref[...]
    la_all = _log_sigmoid(g) * (1.0 / GLA_GATE_NORM)
    for rows in _step_chunks(q_ref, reverse):
        for h in range(GLA_HEADS):
            cols = slice(h * LANES, (h + 1) * LANES)
            q = q_ref[0, rows, cols] * (GLA_DK ** -0.5)
            o_ref[0, rows, cols] = _chunk_scan(q, k_ref[0, rows, cols], v_ref[0, rows, cols], la_all[rows, cols],
                                               st_ref.at[h], reverse)
    _scan_epilogue(sfin_ref, st_ref)


def _scan_chunk(l):
    return SCAN_CHUNK if l % SCAN_CHUNK == 0 else SCAN_CHUNK // 2


def _scan_block(l):
    c = _scan_chunk(l)
    return c * SCAN_STEP_CHUNKS if l % (c * SCAN_STEP_CHUNKS) == 0 else c


def _scan_call(kernel, reverse, batch, l, heads, in_arrays, in_specs, s0):
    c = _scan_block(l)
    n = l // c
    width = heads * LANES
    state_spec = pl.BlockSpec((1, heads, LANES, LANES), lambda b, j: (b, 0, 0, 0))
    return pl.pallas_call(
        functools.partial(kernel, reverse),
        grid=(batch, n),
        in_specs=in_specs + [state_spec],
        out_specs=[pl.BlockSpec((1, c, width), lambda b, j: (b, (n - 1 - j) if reverse else j, 0)),
                   state_spec],
        out_shape=[jax.ShapeDtypeStruct((batch, l, width), F32),
                   jax.ShapeDtypeStruct((batch, heads, LANES, LANES), F32)],
        scratch_shapes=[pltpu.VMEM((heads, LANES, LANES), F32)],
        compiler_params=_cparams(("parallel", "arbitrary")),
    )(*in_arrays, s0)


def hgrn_scan(u, gate_block, consts, s0, reverse):
    b, l, _ = u.shape
    c = _scan_block(l)
    n = l // c
    width = HG_HEADS * LANES

    def col(block):
        return pl.BlockSpec((1, c, width), lambda i, j: (i, (n - 1 - j) if reverse else j, block))

    in_specs = [col(0), col(3), col(gate_block), pl.BlockSpec((3, width), lambda i, j: (0, 0))]
    return _scan_call(_hgrn_scan_kernel, reverse, b, l, HG_HEADS, (u, u, u, consts), in_specs, s0)


def gla_scan(q, k, v, r, w, bias, s0, reverse):
    b, l, width = q.shape
    c = _scan_block(l)
    n = l // c
    rank = r.shape[-1]

    def col(wd):
        return pl.BlockSpec((1, c, wd), lambda i, j: (i, (n - 1 - j) if reverse else j, 0))

    in_specs = [col(width), col(width), col(width), col(rank),
                pl.BlockSpec((rank, width), lambda i, j: (0, 0)),
                pl.BlockSpec((1, width), lambda i, j: (0, 0))]
    return _scan_call(_gla_scan_kernel, reverse, b, l, GLA_HEADS, (q, k, v, r, w, bias), in_specs, s0)


def _head_rms(a, gain, heads):
    outs = []
    for h in range(heads):
        ah = a[:, h * LANES:(h + 1) * LANES]
        outs.append(ah * lax.rsqrt(jnp.mean(ah * ah, axis=-1, keepdims=True) + EPS) * gain)
    return jnp.concatenate(outs, axis=1)


def _mix_out_kernel(alpha, hf_ref, hb_ref, gf_ref, gb_ref, hgate_ref, ggate_ref, x_ref, g1_ref,
                    hgain_ref, ggain_ref, w_ref, gam_ref, bet_ref, o_ref):
    ma = _head_rms(hf_ref[0] + hb_ref[0], hgain_ref[...], HG_HEADS) * _silu(hgate_ref[0])
    mb = _head_rms(gf_ref[0] + gb_ref[0], ggain_ref[...], GLA_HEADS) * _silu(ggate_ref[0])
    mix = jnp.concatenate([ma, mb], axis=1).astype(BF16)
    y = jnp.dot(mix, w_ref[...], preferred_element_type=F32)
    r = alpha * x_ref[0] + g1_ref[0] * y
    o_ref[0] = _layer_norm(r) * gam_ref[...] + bet_ref[...]


def mix_out(alpha, hf, hb, gf, gb, u, x, g1, hgain, ggain, w_out_bf16, gamma, beta, tile):
    b, l, d = x.shape
    hw = HG_HEADS * LANES
    tok = lambda width, blk: pl.BlockSpec((1, tile, width), lambda i, j: (i, j, blk))
    vec = lambda width: pl.BlockSpec((1, width), lambda i, j: (0, 0))
    return pl.pallas_call(
        functools.partial(_mix_out_kernel, alpha),
        grid=(b, l // tile),
        in_specs=[tok(hw, 0), tok(hw, 0), tok(hw, 0), tok(hw, 0),
                  tok(hw, 4), tok(hw, 8), tok(d, 0),
                  pl.BlockSpec((1, 1, d), lambda i, j: (i, 0, 0)),
                  vec(LANES), vec(LANES),
                  pl.BlockSpec(w_out_bf16.shape, lambda i, j: (0, 0)),
                  vec(d), vec(d)],
        out_specs=tok(d, 0),
        out_shape=jax.ShapeDtypeStruct((b, l, d), F32),
        compiler_params=_cparams(("parallel", "parallel")),
    )(hf, hb, gf, gb, u, u, x, g1, hgain, ggain, w_out_bf16, gamma, beta)


_CAND_PIECES = ((0, 16),) + tuple((a, 8) for a in range(1, 8))
_CAND_TAIL = 8
_NCAND = sum(n for _, n in _CAND_PIECES) + _CAND_TAIL


def _split_bf16(x):
    hi = x.astype(BF16)
    return hi, (x - hi.astype(F32)).astype(BF16)


def _dot3_nt(a, b):
    (ah, al), (bh, bl) = a, b
    dot = lambda x, y: lax.dot_general(x, y, _NT, preferred_element_type=F32)
    return dot(ah, bh) + (dot(ah, bl) + dot(al, bh))


def _extract_top(s, iota, nrows):
    m = jnp.max(s, axis=0, keepdims=True)
    pos = jnp.min(jnp.where(s == m, iota, nrows), axis=0, keepdims=True)
    return m, pos, iota == pos


def _peer_route_kernel(q_ref, k1_ref, k2_ref, idx_ref, g_ref, sc, v1, i1, v2, i2, cs, ci, tv):
    tt = q_ref.shape[1]
    nk, topk = PEER_NKEYS, PEER_TOPK
    keys = (_split_bf16(k1_ref[...]), _split_bf16(k2_ref[...]))
    for h in range(PEER_HEADS):
        for side in range(2):
            o = nk * (2 * h + side)
            sc[2 * h + side] = _dot3_nt(keys[side], _split_bf16(q_ref[0, :, o:o + nk]))
    iota = lax.broadcasted_iota(jnp.int32, (nk, tt), 0)
    iota2 = lax.broadcasted_iota(jnp.int32, (_NCAND, tt), 0)

    for h in range(PEER_HEADS + 1):
        first, second = h < PEER_HEADS, h >= 1
        cur, prev = h % 2, (h - 1) % 2
        out0 = (h - 1) * topk
        init = ()
        if first:
            init += (sc[2 * h], sc[2 * h + 1])
        if second:
            row = 0
            for a, n in _CAND_PIECES:
                cs[row:row + n, :] = v1[prev, a:a + 1, :] + v2[prev, 0:n, :]
                ci[row:row + n, :] = i1[prev, a:a + 1, :] * nk + i2[prev, 0:n, :]
                row += n
            cs[row:row + _CAND_TAIL, :] = v1[prev, topk - _CAND_TAIL:topk, :] + v2[prev, 0:1, :]
            ci[row:row + _CAND_TAIL, :] = i1[prev, topk - _CAND_TAIL:topk, :] * nk + i2[prev, 0:1, :]
            init += (cs[...],)

        def step(j, st, first=first, second=second, cur=cur, out0=out0):
            out = ()
            if first:
                s1, s2 = st[0], st[1]
                m1, p1, hit1 = _extract_top(s1, iota, nk)
                m2, p2, hit2 = _extract_top(s2, iota, nk)
                v1[cur, pl.ds(j, 1), :] = m1
                i1[cur, pl.ds(j, 1), :] = p1
                v2[cur, pl.ds(j, 1), :] = m2
                i2[cur, pl.ds(j, 1), :] = p2
                out += (jnp.where(hit1, -jnp.inf, s1), jnp.where(hit2, -jnp.inf, s2))
            if second:
                s = st[-1]
                m, _, hit = _extract_top(s, iota2, _NCAND)
                tv[pl.ds(j, 1), :] = m
                idx_ref[0, pl.ds(out0 + j, 1), :] = jnp.max(jnp.where(hit, ci[...], -1), axis=0, keepdims=True)
                out += (jnp.where(hit, -jnp.inf, s),)
            return out

        lax.fori_loop(0, topk, step, init)
        if second:
            top = tv[...]
            e = jnp.exp(top - top[0:1, :])
            g_ref[0, out0:out0 + topk, :] = e / jnp.sum(e, axis=0, keepdims=True)


def peer_route(q, sub_keys, tile):
    b, l, dq = q.shape
    nslot = PEER_HEADS * PEER_TOPK
    out_spec = pl.BlockSpec((1, nslot, tile), lambda i, j: (i, 0, j))
    keys = pl.BlockSpec((PEER_NKEYS, PEER_NKEYS), lambda i, j: (0, 0))
    return pl.pallas_call(
        _peer_route_kernel,
        grid=(b, l // tile),
        in_specs=[pl.BlockSpec((1, tile, dq), lambda i, j: (i, j, 0)), keys, keys],
        out_specs=[out_spec, out_spec],
        out_shape=[jax.ShapeDtypeStruct((b, nslot, l), jnp.int32),
                   jax.ShapeDtypeStruct((b, nslot, l), F32)],
        scratch_shapes=[pltpu.VMEM((2 * PEER_HEADS, PEER_NKEYS, tile), F32),
                        pltpu.VMEM((2, PEER_TOPK, tile), F32), pltpu.VMEM((2, PEER_TOPK, tile), jnp.int32),
                        pltpu.VMEM((2, PEER_TOPK, tile), F32), pltpu.VMEM((2, PEER_TOPK, tile), jnp.int32),
                        pltpu.VMEM((_NCAND, tile), F32), pltpu.VMEM((_NCAND, tile), jnp.int32),
                        pltpu.VMEM((PEER_TOPK, tile), F32)],
        compiler_params=_cparams(("parallel", "parallel")),
    )(q, sub_keys[0], sub_keys[1])


PEER_TOK_TILE = 8
PEER_ISSUE_SLACK = 2
ROW_WORDS = SUBLANES
PAIR_ROWS = ROW_WORDS // 2


def _unpack_pairs(words):
    lo = pltpu.bitcast(words << 16, F32)
    hi = pltpu.bitcast(words & jnp.int32(-65536), F32)
    return lo, hi


def _expert_token(words, hm, g_ref, y_ref, tok):
    hrow = lambda c: hm[tok:tok + 1, c * LANES:(c + 1) * LANES]
    acc = None
    for j in range(PAIR_ROWS):
        lo, hi = _unpack_pairs(words(j))
        part = lo * hrow(2 * j) + hi * hrow(2 * j + 1)
        acc = part if acc is None else acc + part
    a = jnp.sum(acc, axis=1, keepdims=True)
    w = jax.nn.gelu(a) * g_ref[0, :, tok:tok + 1]
    for j in range(PAIR_ROWS):
        lo, hi = _unpack_pairs(words(PAIR_ROWS + j))
        for c, vc in ((2 * j, lo), (2 * j + 1, hi)):
            y_ref[tok:tok + 1, c * LANES:(c + 1) * LANES] = jnp.sum(vc * w, axis=0, keepdims=True)


def _peer_experts_kernel(alpha, ids_ref, ids_next_ref, g_ref, x_ref, sh_ref, sc_ref, g2_ref, gam_ref, bet_ref,
                         uv_hbm, o_ref, buf_a, buf_b, y_ref, sem):
    tt = PEER_TOK_TILE
    d = x_ref.shape[1]
    nslot = g_ref.shape[1]
    step = pl.program_id(0)
    last = pl.num_programs(0) - 1
    bufs = (buf_a, buf_b)

    def row_copy(expert, dst, row, s):
        return pltpu.make_async_copy(uv_hbm.at[expert], dst.at[pl.ds(row, ROW_WORDS)], sem.at[s])

    def tile_wait(buf, s):
        pltpu.make_async_copy(buf, buf, sem.at[s]).wait()

    @pl.when(step == 0)
    def _():
        def per_token(t, carry):
            def per_group(g, carry):
                for j in range(SUBLANES):
                    k = g * SUBLANES + j
                    row = pl.multiple_of((t * nslot + k) * ROW_WORDS, ROW_WORDS)
                    row_copy(ids_ref[0, t, k], buf_a, row, 0).start(priority=j % 2)
                return carry
            return lax.fori_loop(0, nslot // SUBLANES, per_group, carry)
        lax.fori_loop(0, tt, per_token, 0)

    x = x_ref[...]
    hm = _layer_norm(x) * (1.0 + sc_ref[0]) + sh_ref[0]

    issue_tokens = tt - PEER_ISSUE_SLACK
    per_token_issue = -(-tt * nslot // issue_tokens)

    for phase in range(2):
        cur, nxt = bufs[phase], bufs[1 - phase]
        tile_wait(cur, phase)
        for t in range(tt):
            for n in range(t * per_token_issue, min((t + 1) * per_token_issue, tt * nslot)):
                tn, k = divmod(n, nslot)
                expert = ids_ref[0, tt + tn, k] if phase == 0 else ids_next_ref[0, tn, k]
                row_copy(expert, nxt, n * ROW_WORDS, 1 - phase).start(priority=n % 2)
            tok = phase * tt + t
            base = t * nslot * ROW_WORDS
            words = lambda j, base=base, cur=cur: cur[pl.ds(base + j, nslot, stride=ROW_WORDS), :]
            _expert_token(words, hm, g_ref, y_ref, tok)

    r = alpha * x + g2_ref[0] * y_ref[...]
    o_ref[...] = _layer_norm(r) * gam_ref[...] + bet_ref[...]

    @pl.when(step == last)
    def _():
        tile_wait(buf_a, 0)


def pack_experts(u, v):
    def pack(a):
        bits = lax.bitcast_convert_type(a.astype(BF16).reshape(a.shape[0], PAIR_ROWS, 2, LANES), jnp.uint16)
        bits = bits.astype(jnp.uint32)
        return bits[:, :, 0, :] | (bits[:, :, 1, :] << 16)
    return lax.bitcast_convert_type(jnp.concatenate([pack(u), pack(v)], axis=1), jnp.int32)


def peer_experts(alpha, x, shift, scale, g2, gamma, beta, idx_t, gates_t, uv):
    b, l, d = x.shape
    assert d == 2 * PAIR_ROWS * LANES
    nslot = idx_t.shape[1]
    blk = 2 * PEER_TOK_TILE
    blocks_per_batch = l // blk
    nb = b * blocks_per_batch
    idx = idx_t.transpose(0, 2, 1).reshape(nb, blk, nslot)
    gates = gates_t.reshape(b, nslot, blocks_per_batch, blk).transpose(0, 2, 1, 3).reshape(nb, nslot, blk)
    per_batch = lambda i: (i // blocks_per_batch, 0, 0)
    vec = pl.BlockSpec((1, d), lambda i: (0, 0))
    gather_buf = pltpu.VMEM((PEER_TOK_TILE * nslot * ROW_WORDS, LANES), jnp.int32)
    out = pl.pallas_call(
        functools.partial(_peer_experts_kernel, alpha),
        grid=(nb,),
        in_specs=[pl.BlockSpec((1, blk, nslot), lambda i: (i, 0, 0), memory_space=pltpu.SMEM),
                  pl.BlockSpec((1, blk, nslot), lambda i: (jnp.minimum(i + 1, nb - 1), 0, 0),
                               memory_space=pltpu.SMEM),
                  pl.BlockSpec((1, nslot, blk), lambda i: (i, 0, 0)),
                  pl.BlockSpec((blk, d), lambda i: (i, 0)),
                  pl.BlockSpec((1, 1, d), per_batch),
                  pl.BlockSpec((1, 1, d), per_batch),
                  pl.BlockSpec((1, 1, d), per_batch),
                  vec, vec,
                  pl.BlockSpec(memory_space=pl.ANY)],
        out_specs=pl.BlockSpec((blk, d), lambda i: (i, 0)),
        out_shape=jax.ShapeDtypeStruct((b * l, d), F32),
        scratch_shapes=[gather_buf, gather_buf, pltpu.VMEM((blk, d), F32), pltpu.SemaphoreType.DMA((2,))],
        compiler_params=_cparams(("arbitrary",)),
    )(idx, idx, gates, x.reshape(b * l, d), shift, scale, g2, gamma, beta, uv)
    return out.reshape(b, l, d)


def _to_col_major(t, rows):
    b, l, c = t.shape
    return t.reshape(b, rows, GRID_W, c).transpose(0, 2, 1, 3).reshape(b, l, c)


def _from_col_major(t, rows):
    b, l, c = t.shape
    return t.reshape(b, GRID_W, rows, c).transpose(0, 2, 1, 3).reshape(b, l, c)


def _pad_heads(t):
    b, l, _ = t.shape
    t = t.reshape(b, l, GLA_HEADS, GLA_DK)
    t = jnp.pad(t, ((0, 0), (0, 0), (0, 0), (0, LANES - GLA_DK)))
    return t.reshape(b, l, GLA_HEADS * LANES)


_HG_WIDTH = 5 * HG_HEADS * LANES
_GLA_WIDTH = GLA_HEADS * LANES


def _project_weights(w):
    o, kw = _HG_WIDTH, GLA_HEADS * GLA_DK
    pad_cols = lambda cols: _pad_heads(cols[None])[0]
    w = jnp.concatenate([w[:, :o], pad_cols(w[:, o:o + kw]), pad_cols(w[:, o + kw:o + 2 * kw]),
                         w[:, o + 2 * kw:]], axis=1)
    return jnp.pad(w, ((0, 0), (0, -w.shape[1] % LANES))).astype(BF16)


def _gla_parts(u, rows):
    o, gw = _HG_WIDTH, _GLA_WIDTH
    order = (lambda t: t) if rows is None else (lambda t: _to_col_major(t, rows))
    q, k, v = (order(u[..., o + n * gw:o + (n + 1) * gw]) for n in range(3))
    r0 = o + 4 * gw
    rf = order(u[..., r0:r0 + GLA_GATE_RANK])
    rb = order(u[..., r0 + GLA_GATE_RANK:r0 + 2 * GLA_GATE_RANK])
    return q, k, v, rf, rb


def _token_tile(l, want):
    while l % want:
        want //= 2
    return want


def kernel(x, c, ctx, c_ctx, w_ada, b_ada, w_in, w_gk2, b_gk, hg_lower_bounds, hg_norm, gla_norm, w_out,
           ln_gamma, ln_beta, peer_w_query, peer_sub_keys, peer_u, peer_v):
    batch, seq, d = x.shape
    depth = w_ada.shape[0]
    rows = seq // GRID_W
    alpha = (2.0 * depth) ** 0.25

    sm = jax.nn.softmax(hg_lower_bounds.astype(F32), axis=1)
    lb_all = jnp.clip(jnp.cumsum(sm, axis=1) - sm[:, :1], 0.0, 1.0 - 1e-6)
    hg_consts = jnp.stack([jnp.log(jnp.maximum(lb_all, LB_FLOOR)), jnp.log1p(-lb_all), 1.0 - lb_all], axis=2)

    cs = jnp.concatenate([c, c_ctx[None, :], jnp.zeros((SUBLANES - 1 - batch % SUBLANES, d), F32)], axis=0)
    mods = ada_mods(cs, w_ada, b_ada)

    zero_state = jnp.zeros((batch, HG_HEADS, LANES, LANES), F32)
    lat_tile = _token_tile(seq, 256)
    ctx_tile = _token_tile(ctx.shape[1], 256)
    gw_pad = lambda w: _pad_heads(w[None])[0]

    xc = ctx
    for i in range(depth):
        last = i == depth - 1
        m_l = mods[i, :batch].reshape(batch, 1, 6, d)
        m_c = jnp.broadcast_to(mods[i, batch].reshape(1, 1, 6, d), (batch, 1, 6, d))
        w_in_b = _project_weights(w_in[i])
        w_out_b = w_out[i].astype(BF16)
        w_q_b = peer_w_query[i].astype(BF16)
        gam, bet = ln_gamma[i], ln_beta[i]
        hgain, ggain = hg_norm[i][None, :], gla_norm[i][None, :]
        gk_w = [gw_pad(w_gk2[i, dd]) for dd in range(2)]
        gk_b = [gw_pad(b_gk[i, dd][None, :]) for dd in range(2)]
        uv = pack_experts(peer_u[i], peer_v[i])

        u_c = ln_mod_matmul(xc, m_c[:, :, 0], m_c[:, :, 1], w_in_b, ctx_tile)
        u_l = ln_mod_matmul(x, m_l[:, :, 0], m_l[:, :, 1], w_in_b, lat_tile)
        h_out, g_out = {}, {}
        gc = _gla_parts(u_c, None)
        gl = _gla_parts(u_l, rows)
        for dd, rev in ((0, False), (1, True)):
            o_c, s_c = hgrn_scan(u_c, 1 + dd, hg_consts[dd, i], zero_state, rev)
            o_l, _ = hgrn_scan(u_l, 1 + dd, hg_consts[dd, i], s_c, rev)
            h_out[dd] = (o_c, o_l)
            o_c, s_c = gla_scan(gc[0], gc[1], gc[2], gc[3 + dd], gk_w[dd], gk_b[dd], zero_state, rev)
            o_l, _ = gla_scan(gl[0], gl[1], gl[2], gl[3 + dd], gk_w[dd], gk_b[dd], s_c, rev)
            g_out[dd] = (o_c, _from_col_major(o_l, rows))
        x = mix_out(alpha, h_out[0][1], h_out[1][1], g_out[0][1], g_out[1][1], u_l, x, m_l[:, :, 2],
                    hgain, ggain, w_out_b, gam[0:1], bet[0:1], lat_tile)
        if not last:
            xc = mix_out(alpha, h_out[0][0], h_out[1][0], g_out[0][0], g_out[1][0], u_c, xc, m_c[:, :, 2],
                         hgain, ggain, w_out_b, gam[0:1], bet[0:1], ctx_tile)

        def peer_block(h, m, tile):
            qh = ln_mod_matmul(h, m[:, :, 3], m[:, :, 4], w_q_b, tile)
            idx_t, gates_t = peer_route(qh, peer_sub_keys[i], _token_tile(h.shape[1], LANES))
            return peer_experts(alpha, h, m[:, :, 3], m[:, :, 4], m[:, :, 5], gam[1:2], bet[1:2], idx_t, gates_t, uv)

        x = peer_block(x, m_l, lat_tile)
        if not last:
            xc = peer_block(xc, m_c, ctx_tile)
    return x
```
